```python
import jax, jax.numpy as jnp
from jax import lax
import numpy as np

D_MODEL = 1024
BATCH = 4
SEQ = 4096
DEPTH = 1
DEC_BATCH = 128
DEC_SEQ = 8
PAST_LEN = 2048
PAGE_SIZE = 128

D_CONV = D_MODEL // 2
CONV_W = 3
N_HEADS = 8
HEAD_DIM = D_MODEL // N_HEADS
D_ATTN = N_HEADS * HEAD_DIM
MOBA_BLOCK = 256
MOBA_TOPK = 3
Q_CHUNK = 64
N_MEM = 256
MEM_HEADS = 4
MEM_HEAD_DIM = 128
D_MEM = MEM_HEADS * MEM_HEAD_DIM
PEER_HEADS = 8
PEER_NKEYS = 128
PEER_N_EXPERTS = PEER_NKEYS * PEER_NKEYS
PEER_DKEY = 128
PEER_TOPK = 16
TOKEN_CHUNK = 256
EPS = 1e-6
IN_WIDTHS = (D_CONV, D_CONV, D_CONV, D_ATTN, D_ATTN, D_ATTN, D_MEM, D_MODEL, D_MODEL, D_MODEL)
D_IN = sum(IN_WIDTHS)

kernel_name = "hybrid_conv_moba_mem_peer_step"


def rmsnorm(x, g):
    xf = x.astype(jnp.float32)
    r = lax.rsqrt(jnp.mean(xf * xf, axis=-1, keepdims=True) + EPS)
    return (xf * r * g.astype(jnp.float32)).astype(x.dtype)


def short_conv(u, state, conv_w):
    T = u.shape[1]
    ext = jnp.concatenate([state, u], axis=1)
    y = sum(conv_w[j] * ext[:, j:j + T] for j in range(CONV_W))
    return y, ext[:, T:]


def moba_attention(q, k_all, v_all, q_pos):
    N, T, H, Dh = q.shape
    L = k_all.shape[1]
    nb = -(-L // MOBA_BLOCK)
    pad = nb * MOBA_BLOCK - L
    kb = jnp.pad(k_all, ((0, 0), (0, pad), (0, 0), (0, 0))).reshape(N, nb, MOBA_BLOCK, H, Dh)
    vb = jnp.pad(v_all, ((0, 0), (0, pad), (0, 0), (0, 0))).reshape(N, nb, MOBA_BLOCK, H, Dh)
    k_mean = jnp.mean(kb.astype(jnp.float32), axis=2)
    k_sel = min(MOBA_TOPK, nb)
    n_items = N * T
    n_pad = (-n_items) % Q_CHUNK
    q_it = jnp.pad(q.reshape(n_items, H, Dh), ((0, n_pad), (0, 0), (0, 0)))
    b_it = jnp.pad(jnp.repeat(jnp.arange(N, dtype=jnp.int32), T), (0, n_pad))
    p_it = jnp.pad(jnp.tile(q_pos.astype(jnp.int32), N), (0, n_pad))
    n_chunks = (n_items + n_pad) // Q_CHUNK
    scale = HEAD_DIM ** -0.5
    key_off = jnp.arange(MOBA_BLOCK, dtype=jnp.int32)
    blk_ids = jnp.arange(nb, dtype=jnp.int32)
    h_idx = jnp.arange(H, dtype=jnp.int32)[None, :, None]

    def chunk(args):
        qc, bc, pc = args
        own = pc // MOBA_BLOCK
        gate = jnp.einsum('qhd,qnhd->qhn', qc.astype(jnp.float32), k_mean[bc])
        gate = jnp.where(blk_ids[None, None, :] < own[:, None, None], gate, -jnp.inf)
        _, sel = lax.top_k(gate, k_sel)
        sel_valid = jnp.broadcast_to(jnp.arange(k_sel)[None, None, :] < own[:, None, None], sel.shape)
        own_b = jnp.broadcast_to(own[:, None, None], (Q_CHUNK, H, 1))
        blocks = jnp.concatenate([sel.astype(jnp.int32), own_b], axis=-1)
        valid = jnp.concatenate([sel_valid, jnp.ones((Q_CHUNK, H, 1), bool)], axis=-1)
        kg = kb[bc[:, None, None], blocks, :, h_idx]
        vg = vb[bc[:, None, None], blocks, :, h_idx]
        key_pos = blocks[..., None] * MOBA_BLOCK + key_off
        mask = valid[..., None] & (key_pos <= pc[:, None, None, None])
        s = jnp.einsum('qhd,qhjkd->qhjk', qc, kg).astype(jnp.float32) * scale
        s = jnp.where(mask, s, -jnp.inf).reshape(Q_CHUNK, H, -1)
        p = jax.nn.softmax(s, axis=-1).reshape(Q_CHUNK, H, k_sel + 1, MOBA_BLOCK).astype(vg.dtype)
        return jnp.einsum('qhjk,qhjkd->qhd', p, vg)

    out = lax.map(chunk, (q_it.reshape(n_chunks, Q_CHUNK, H, Dh),
                          b_it.reshape(n_chunks, Q_CHUNK),
                          p_it.reshape(n_chunks, Q_CHUNK)))
    return out.reshape(-1, H, Dh)[:n_items].reshape(N, T, H, Dh)


def memory_kv(mem, mem_norm, w_mem_kv, mk_norm):
    N, M, _ = mem.shape
    kv = rmsnorm(mem, mem_norm) @ w_mem_kv
    k, v = jnp.split(kv, 2, axis=-1)
    k = rmsnorm(k.reshape(N, M, MEM_HEADS, MEM_HEAD_DIM), mk_norm)
    return k, v.reshape(N, M, MEM_HEADS, MEM_HEAD_DIM)


def token_mixers(x, q_pos, k_past, v_past, conv_state, mem_k, mem_v,
                 attn_norm, w_in, conv_w, w_conv_out, q_norm, k_norm, w_attn_out,
                 mq_norm, w_mem_out, w_o):
    N, T, _ = x.shape
    xn = rmsnorm(x, attn_norm)
    proj = xn @ w_in
    offs = np.cumsum(IN_WIDTHS)[:-1].tolist()
    cb, cc, ch, q, k, v, mq, g_conv, g_attn, g_mem = jnp.split(proj, offs, axis=-1)
    y_conv, new_conv = short_conv(cc * ch, conv_state, conv_w)
    conv_out = (cb * y_conv) @ w_conv_out
    q = rmsnorm(q.reshape(N, T, N_HEADS, HEAD_DIM), q_norm)
    k = rmsnorm(k.reshape(N, T, N_HEADS, HEAD_DIM), k_norm)
    v = v.reshape(N, T, N_HEADS, HEAD_DIM)
    k_all = jnp.concatenate([k_past, k], axis=1)
    v_all = jnp.concatenate([v_past, v], axis=1)
    o = moba_attention(q, k_all, v_all, q_pos)
    attn_out = o.reshape(N, T, D_ATTN) @ w_attn_out
    mq = rmsnorm(mq.reshape(N, T, MEM_HEADS, MEM_HEAD_DIM), mq_norm)
    s = jnp.einsum('bthd,bmhd->bhtm', mq, mem_k).astype(jnp.float32) * (MEM_HEAD_DIM ** -0.5)
    p = jax.nn.softmax(s, axis=-1).astype(mem_v.dtype)
    om = jnp.einsum('bhtm,bmhd->bthd', p, mem_v)
    mem_out = om.reshape(N, T, D_MEM) @ w_mem_out
    merged = (jax.nn.sigmoid(g_conv) * conv_out + jax.nn.sigmoid(g_attn) * attn_out
              + jax.nn.sigmoid(g_mem) * mem_out)
    return x + merged @ w_o, k, v, new_conv


def peer_ffn(x, ffn_norm, peer_wq, peer_subkeys, peer_u, peer_v):
    N, T, D = x.shape
    n_tok = N * T
    n_pad = (-n_tok) % TOKEN_CHUNK
    xn = jnp.pad(rmsnorm(x, ffn_norm).reshape(n_tok, D), ((0, n_pad), (0, 0)))
    xn = xn.reshape(-1, TOKEN_CHUNK, D)

    def chunk(xc):
        qh = (xc @ peer_wq).reshape(TOKEN_CHUNK, PEER_HEADS, 2, PEER_DKEY)
        s = jnp.einsum('thcd,hckd->thck', qh, peer_subkeys).astype(jnp.float32)
        s_top, i_top = lax.top_k(s, PEER_TOPK)
        cand = s_top[..., 0, :, None] + s_top[..., 1, None, :]
        cand_idx = i_top[..., 0, :, None] * PEER_NKEYS + i_top[..., 1, None, :]
        best, pos = lax.top_k(cand.reshape(TOKEN_CHUNK, PEER_HEADS, -1), PEER_TOPK)
        idx = jnp.take_along_axis(cand_idx.reshape(TOKEN_CHUNK, PEER_HEADS, -1), pos, axis=-1)
        g = jax.nn.softmax(best, axis=-1).astype(xc.dtype)
        a = jax.nn.gelu(jnp.einsum('thed,td->the', peer_u[idx], xc), approximate=False)
        return jnp.einsum('the,thed->td', g * a, peer_v[idx])

    y = lax.map(chunk, xn).reshape(-1, D)[:n_tok].reshape(N, T, D)
    return x + y


def setup_inputs(seed: int = 0) -> dict:
    key = jax.random.key(seed)
    ks = jax.random.split(key, 32)
    f32 = jnp.float32
    n_pages = PAST_LEN // PAGE_SIZE
    n_used = DEC_BATCH * n_pages
    n_pool = n_used + max(1, n_used // 4)
    nrm = lambda k, shape, s: jax.random.normal(k, shape, f32) * s
    gain = lambda k, shape: 1.0 + 0.1 * jax.random.normal(k, shape, f32)
    page_table = jax.random.permutation(ks[0], n_pool)[:n_used].reshape(DEC_BATCH, n_pages).astype(jnp.int32)
    return {
        'x_prompt': nrm(ks[1], (BATCH, SEQ, D_MODEL), 1.0),
        'x_sample': nrm(ks[2], (DEC_BATCH, DEC_SEQ, D_MODEL), 1.0),
        'mem_prompt': nrm(ks[3], (BATCH, N_MEM, D_MODEL), 1.0),
        'cache_k': nrm(ks[4], (DEPTH, n_pool, PAGE_SIZE, N_HEADS, HEAD_DIM), 1.0),
        'cache_v': nrm(ks[5], (DEPTH, n_pool, PAGE_SIZE, N_HEADS, HEAD_DIM), 1.0),
        'cache_conv': nrm(ks[6], (DEPTH, DEC_BATCH, CONV_W - 1, D_CONV), 0.5),
        'cache_mem_k': nrm(ks[7], (DEPTH, DEC_BATCH, N_MEM, MEM_HEADS, MEM_HEAD_DIM), 1.0),
        'cache_mem_v': nrm(ks[8], (DEPTH, DEC_BATCH, N_MEM, MEM_HEADS, MEM_HEAD_DIM), 1.0),
        'page_table': page_table,
        'attn_norm': gain(ks[9], (DEPTH, D_MODEL)),
        'w_in': nrm(ks[10], (DEPTH, D_MODEL, D_IN), D_MODEL ** -0.5),
        'conv_w': nrm(ks[11], (DEPTH, CONV_W, D_CONV), CONV_W ** -0.5),
        'w_conv_out': nrm(ks[12], (DEPTH, D_CONV, D_MODEL), D_CONV ** -0.5),
        'q_norm': gain(ks[13], (DEPTH, HEAD_DIM)),
        'k_norm': gain(ks[14], (DEPTH, HEAD_DIM)),
        'w_attn_out': nrm(ks[15], (DEPTH, D_ATTN, D_MODEL), D_ATTN ** -0.5),
        'mem_norm': gain(ks[16], (DEPTH, D_MODEL)),
        'w_mem_kv': nrm(ks[17], (DEPTH, D_MODEL, 2 * D_MEM), D_MODEL ** -0.5),
        'mk_norm': gain(ks[18], (DEPTH, MEM_HEAD_DIM)),
        'mq_norm': gain(ks[19], (DEPTH, MEM_HEAD_DIM)),
        'w_mem_out': nrm(ks[20], (DEPTH, D_MEM, D_MODEL), D_MEM ** -0.5),
        'w_o': nrm(ks[21], (DEPTH, D_MODEL, D_MODEL), D_MODEL ** -0.5),
        'ffn_norm': gain(ks[22], (DEPTH, D_MODEL)),
        'peer_wq': nrm(ks[23], (DEPTH, D_MODEL, PEER_HEADS * 2 * PEER_DKEY), D_MODEL ** -0.5),
        'peer_subkeys': nrm(ks[24], (DEPTH, PEER_HEADS, 2, PEER_NKEYS, PEER_DKEY), PEER_DKEY ** -0.5),
        'peer_u': nrm(ks[25], (DEPTH, PEER_N_EXPERTS, D_MODEL), D_MODEL ** -0.5),
        'peer_v': nrm(ks[26], (DEPTH, PEER_N_EXPERTS, D_MODEL), 0.25),
    }


def reference(x_prompt, x_sample, mem_prompt, cache_k, cache_v, cache_conv, cache_mem_k, cache_mem_v,
              page_table, attn_norm, w_in, conv_w, w_conv_out, q_norm, k_norm, w_attn_out,
              mem_norm, w_mem_kv, mk_norm, mq_norm, w_mem_out, w_o,
              ffn_norm, peer_wq, peer_subkeys, peer_u, peer_v):
    B, S, _ = x_prompt.shape
    DB, DS, _ = x_sample.shape
    past = page_table.shape[1] * PAGE_SIZE
    pos_prompt = jnp.arange(S, dtype=jnp.int32)
    pos_sample = PAST_LEN + jnp.arange(DS, dtype=jnp.int32)
    h_p, h_s = x_prompt, x_sample
    kp_l, vp_l, cp_l, mkp_l, mvp_l, ks_l, vs_l, cs_l = [], [], [], [], [], [], [], []
    for l in range(DEPTH):
        mix_w = (attn_norm[l], w_in[l], conv_w[l], w_conv_out[l], q_norm[l], k_norm[l], w_attn_out[l],
                 mq_norm[l], w_mem_out[l], w_o[l])
        ffn_w = (ffn_norm[l], peer_wq[l], peer_subkeys[l], peer_u[l], peer_v[l])
        mem_k_p, mem_v_p = memory_kv(mem_prompt, mem_norm[l], w_mem_kv[l], mk_norm[l])
        empty = jnp.zeros((B, 0, N_HEADS, HEAD_DIM), h_p.dtype)
        conv0 = jnp.zeros((B, CONV_W - 1, D_CONV), h_p.dtype)
        h_p, k_p, v_p, conv_p = token_mixers(h_p, pos_prompt, empty, empty, conv0, mem_k_p, mem_v_p, *mix_w)
        h_p = peer_ffn(h_p, *ffn_w)
        k_past = cache_k[l][page_table].reshape(DB, past, N_HEADS, HEAD_DIM)
        v_past = cache_v[l][page_table].reshape(DB, past, N_HEADS, HEAD_DIM)
        h_s, k_s, v_s, conv_s = token_mixers(h_s, pos_sample, k_past, v_past, cache_conv[l],
                                             cache_mem_k[l], cache_mem_v[l], *mix_w)
        h_s = peer_ffn(h_s, *ffn_w)
        kp_l.append(k_p); vp_l.append(v_p); cp_l.append(conv_p); mkp_l.append(mem_k_p); mvp_l.append(mem_v_p)
        ks_l.append(k_s); vs_l.append(v_s); cs_l.append(conv_s)
    return (h_p, h_s, jnp.stack(kp_l), jnp.stack(vp_l), jnp.stack(cp_l), jnp.stack(mkp_l), jnp.stack(mvp_l),
            jnp.stack(ks_l), jnp.stack(vs_l), jnp.stack(cs_l))
```

```python
import functools

import jax
import jax.numpy as jnp
from jax import lax
from jax.experimental import pallas as pl
from jax.experimental.pallas import tpu as pltpu

F32 = jnp.float32
BF16 = jnp.bfloat16
HIGHEST = lax.Precision.HIGHEST
NEG_INF = float("-inf")

SUBLANES = 8
LANES = 128
VMEM_LIMIT_BYTES = 56 * 1024 * 1024

EPS = 1e-6
D_MODEL = 1024
D_CONV = 512
CONV_W = 3
N_HEADS = 8
HEAD_DIM = 128
D_ATTN = N_HEADS * HEAD_DIM
MOBA_BLOCK = 256
MOBA_TOPK = 3
N_MEM = 256
MEM_HEADS = 4
MEM_HEAD_DIM = 128
D_MEM = MEM_HEADS * MEM_HEAD_DIM
PEER_HEADS = 8
PEER_NKEYS = 128
PEER_DKEY = 128
PEER_TOPK = 16
PEER_N_EXPERTS = PEER_NKEYS * PEER_NKEYS
D_PEER_Q = PEER_HEADS * 2 * PEER_DKEY
D_IN = 3 * D_CONV + 3 * D_ATTN + D_MEM + 3 * D_MODEL
OFF_CB, OFF_CC, OFF_CH = 0, D_CONV, 2 * D_CONV
OFF_Q = 3 * D_CONV
OFF_K = OFF_Q + D_ATTN
OFF_V = OFF_K + D_ATTN
OFF_MQ = OFF_V + D_ATTN
OFF_G = OFF_MQ + D_MEM

TOKEN_TILE = 256
ROUTE_TILE = 256
WBUILD_TILE = 128
PEER_TOKEN_TILE = 512
PEER_EXPERT_CHUNK = 2048
PEER_EXPERT_SUB = 512


def _nt_dot(a, b, precision=None):
    return lax.dot_general(a, b, (((1,), (1,)), ((), ())), precision=precision,
                           preferred_element_type=F32)


def _params(n_grid_dims):
    return pltpu.CompilerParams(dimension_semantics=("arbitrary",) * n_grid_dims,
                                vmem_limit_bytes=VMEM_LIMIT_BYTES)


def _rms(x, gain):
    r = lax.rsqrt(jnp.mean(x * x, axis=-1, keepdims=True) + EPS)
    return x * r * gain


def _resident(shape):
    nd = len(shape)
    return pl.BlockSpec(shape, lambda *_: (0,) * nd, pipeline_mode=pl.Buffered(1))


def _inproj_body(x_ref, an_ref, w_ref, qn_ref, kn_ref, mqn_ref, cw_ref, s0_ref, s1_ref,
                 cy_ref, u_ref, q_ref, k_ref, v_ref, mq_ref, gate_ref, kmean_ref,
                 ubuf_ref, *, tiles_per_seq, seq_len):
    tm = x_ref.shape[0]
    xn = _rms(x_ref[...], an_ref[...]).astype(BF16)

    def proj(off, width):
        return jnp.dot(xn, w_ref[:, off:off + width], preferred_element_type=F32)

    cb = proj(OFF_CB, D_CONV)
    u = proj(OFF_CC, D_CONV) * proj(OFF_CH, D_CONV)
    u_ref[...] = u
    if seq_len >= tm:
        @pl.when(pl.program_id(0) % tiles_per_seq == 0)
        def _():
            ubuf_ref[0:SUBLANES, :] = jnp.zeros((SUBLANES, D_CONV), F32)
    else:
        ubuf_ref[0:SUBLANES, :] = jnp.zeros((SUBLANES, D_CONV), F32)
    ubuf_ref[SUBLANES:SUBLANES + tm, :] = u
    um1 = ubuf_ref[SUBLANES - 1:SUBLANES - 1 + tm, :]
    um2 = ubuf_ref[SUBLANES - 2:SUBLANES - 2 + tm, :]
    if seq_len >= tm:
        ubuf_ref[0:SUBLANES, :] = ubuf_ref[tm:tm + SUBLANES, :]
    else:
        assert seq_len == SUBLANES
        tl = lax.broadcasted_iota(jnp.int32, (tm, D_CONV), 0) % seq_len
        s0 = s0_ref[...]
        s1 = s1_ref[...]
        um1 = jnp.where(tl == 0, s1, um1)
        um2 = jnp.where(tl == 0, s0, jnp.where(tl == 1, s1, um2))
    cw = cw_ref[...]
    y = cw[0:1, :] * um2 + cw[1:2, :] * um1 + cw[2:3, :] * u
    cy_ref[...] = (cb * y).astype(BF16)

    qf = proj(OFF_Q, D_ATTN)
    kf = proj(OFF_K, D_ATTN)
    qn = qn_ref[...]
    kn = kn_ref[...]
    for h in range(N_HEADS):
        sl = slice(h * HEAD_DIM, (h + 1) * HEAD_DIM)
        q_ref[:, sl] = _rms(qf[:, sl], qn)
        k_ref[:, sl] = _rms(kf[:, sl], kn)
    kmean_ref[0] = jnp.mean(k_ref[...], axis=0, keepdims=True)
    v_ref[...] = proj(OFF_V, D_ATTN)
    mqf = proj(OFF_MQ, D_MEM)
    mqn = mqn_ref[...]
    for h in range(MEM_HEADS):
        sl = slice(h * MEM_HEAD_DIM, (h + 1) * MEM_HEAD_DIM)
        mq_ref[:, sl] = _rms(mqf[:, sl], mqn)
    gate_ref[...] = proj(OFF_G, 3 * D_MODEL)


def _inproj(x2, attn_norm, w_in, q_norm, k_norm, mq_norm, conv_w, s0x, s1x, *, seq_len):
    n_tok = x2.shape[0]
    tm = min(TOKEN_TILE, n_tok)
    assert n_tok % tm == 0 and (seq_len % tm == 0 or seq_len == SUBLANES)
    n_tiles = n_tok // tm
    tiles_per_seq = max(seq_len // tm, 1)
    long_seq = seq_len >= tm

    def rows(width):
        return pl.BlockSpec((tm, width), lambda i: (i, 0))

    state_spec = _resident(s0x.shape) if long_seq else rows(D_CONV)
    out_shape = (
        jax.ShapeDtypeStruct((n_tok, D_CONV), BF16),
        jax.ShapeDtypeStruct((n_tok, D_CONV), F32),
        jax.ShapeDtypeStruct((n_tok, D_ATTN), F32),
        jax.ShapeDtypeStruct((n_tok, D_ATTN), F32),
        jax.ShapeDtypeStruct((n_tok, D_ATTN), F32),
        jax.ShapeDtypeStruct((n_tok, D_MEM), F32),
        jax.ShapeDtypeStruct((n_tok, 3 * D_MODEL), F32),
        jax.ShapeDtypeStruct((n_tiles, 1, D_ATTN), F32),
    )
    return pl.pallas_call(
        functools.partial(_inproj_body, tiles_per_seq=tiles_per_seq, seq_len=seq_len),
        grid=(n_tiles,),
        in_specs=[rows(D_MODEL), _resident((1, D_MODEL)), _resident((D_MODEL, D_IN)),
                  _resident((1, HEAD_DIM)), _resident((1, HEAD_DIM)), _resident((1, MEM_HEAD_DIM)),
                  _resident((CONV_W, D_CONV)), state_spec, state_spec],
        out_specs=(rows(D_CONV), rows(D_CONV), rows(D_ATTN), rows(D_ATTN), rows(D_ATTN),
                   rows(D_MEM), rows(3 * D_MODEL),
                   pl.BlockSpec((1, 1, D_ATTN), lambda i: (i, 0, 0))),
        out_shape=out_shape,
        scratch_shapes=[pltpu.VMEM((tm + 2 * SUBLANES, D_CONV), F32)],
        compiler_params=_params(1),
        name="inproj",
    )(x2, attn_norm, w_in, q_norm, k_norm, mq_norm, conv_w, s0x, s1x)


def _moba_prompt_body(q_ref, k_ref, v_ref, km_ref, o_ref,
                      kb_ref, vb_ref, sel_ref, m_ref, l_ref, acc_ref, *, n_blocks):
    b = pl.program_id(2)
    blk = MOBA_BLOCK
    scale = HEAD_DIM ** -0.5

    @pl.when(b == 0)
    def _():
        kb_ref[...] = k_ref[...].astype(BF16)
        vb_ref[...] = v_ref[...].astype(BF16)

    q = q_ref[...]
    gate = _nt_dot(q, km_ref[0], precision=HIGHEST)
    lane = lax.broadcasted_iota(jnp.int32, gate.shape, 1)
    rank = jnp.zeros(gate.shape, F32)
    for jp in range(n_blocks - 1):
        col = gate[:, jp:jp + 1]
        beats = (col > gate) | ((col == gate) & (jp < lane))
        rank = rank + jnp.where(beats, jnp.where(jp < b, 1.0, 0.0), 0.0)
    k_sel = min(MOBA_TOPK, n_blocks)
    sel_ref[...] = jnp.where((rank < k_sel) & (lane < b), 1.0, 0.0)

    qb = q.astype(BF16)
    start = pl.multiple_of(b * blk, blk)
    s = _nt_dot(qb, kb_ref[pl.ds(start, blk), :]) * scale
    row = lax.broadcasted_iota(jnp.int32, s.shape, 0)
    col = lax.broadcasted_iota(jnp.int32, s.shape, 1)
    s = jnp.where(col <= row, s, NEG_INF)
    m = jnp.max(s, axis=1, keepdims=True)
    p = jnp.exp(s - m)
    m_ref[...] = m
    l_ref[...] = jnp.sum(p, axis=1, keepdims=True)
    acc_ref[...] = jnp.dot(p.astype(BF16), vb_ref[pl.ds(start, blk), :], preferred_element_type=F32)

    for j in range(n_blocks - 1):
        @pl.when(j < b)
        def _(j=j):
            sj = _nt_dot(qb, kb_ref[j * blk:(j + 1) * blk, :]) * scale
            sj = jnp.where(sel_ref[:, j:j + 1] > 0.5, sj, NEG_INF)
            m_old = m_ref[...]
            m_new = jnp.maximum(m_old, jnp.max(sj, axis=1, keepdims=True))
            alpha = jnp.exp(m_old - m_new)
            pj = jnp.exp(sj - m_new)
            l_ref[...] = alpha * l_ref[...] + jnp.sum(pj, axis=1, keepdims=True)
            acc_ref[...] = alpha * acc_ref[...] + jnp.dot(
                pj.astype(BF16), vb_ref[j * blk:(j + 1) * blk, :], preferred_element_type=F32)
            m_ref[...] = m_new

    o_ref[...] = (acc_ref[...] / l_ref[...]).astype(o_ref.dtype)


def _moba_prompt(q2, k2, v2, kmean, *, n_seq, seq_len):
    blk = MOBA_BLOCK
    assert seq_len % blk == 0 and TOKEN_TILE == blk
    nb = seq_len // blk
    km = kmean.reshape(n_seq, nb, D_ATTN)
    qo_spec = pl.BlockSpec((blk, HEAD_DIM), lambda n, h, b: (n * nb + b, h))
    kv_spec = pl.BlockSpec((seq_len, HEAD_DIM), lambda n, h, b: (n, h))
    return pl.pallas_call(
        functools.partial(_moba_prompt_body, n_blocks=nb),
        grid=(n_seq, N_HEADS, nb),
        in_specs=[qo_spec, kv_spec, kv_spec,
                  pl.BlockSpec((1, nb, HEAD_DIM), lambda n, h, b: (n, 0, h))],
        out_specs=qo_spec,
        out_shape=jax.ShapeDtypeStruct(q2.shape, BF16),
        scratch_shapes=[pltpu.VMEM((seq_len, HEAD_DIM), BF16), pltpu.VMEM((seq_len, HEAD_DIM), BF16),
                        pltpu.VMEM((blk, nb), F32), pltpu.VMEM((blk, 1), F32),
                        pltpu.VMEM((blk, 1), F32), pltpu.VMEM((blk, HEAD_DIM), F32)],
        compiler_params=_params(3),
        name="moba_prompt",
    )(q2, k2, v2, km)


def _moba_sample_body(pt_ref, q_ref, kn_ref, vn_ref, *rest, n_pages, page_size, n_new):
    del pt_ref
    kp = rest[:n_pages]
    vp = rest[n_pages:2 * n_pages]
    o_ref, s_ref = rest[2 * n_pages:]
    rows = n_new * N_HEADS
    page_rows = page_size * N_HEADS
    pages_per_block = MOBA_BLOCK // page_size
    n_past_blocks = n_pages // pages_per_block
    scale = HEAD_DIM ** -0.5

    q = q_ref[...]
    kmeans = []
    for blk in range(n_past_blocks):
        acc = jnp.zeros((N_HEADS, HEAD_DIM), F32)
        for p in range(blk * pages_per_block, (blk + 1) * pages_per_block):
            acc = acc + jnp.sum(kp[p][...], axis=0)
        kmeans.append(acc * (1.0 / MOBA_BLOCK))
    kmean = jnp.concatenate(kmeans, axis=0)
    gate = _nt_dot(q, kmean, precision=HIGHEST)
    r_head = lax.broadcasted_iota(jnp.int32, gate.shape, 0) % N_HEADS
    c_idx = lax.broadcasted_iota(jnp.int32, gate.shape, 1)
    c_blk = c_idx // N_HEADS
    same_head = (c_idx % N_HEADS) == r_head
    rank = jnp.zeros(gate.shape, F32)
    for bp in range(n_past_blocks):
        colval = jnp.max(jnp.where(same_head & (c_blk == bp), gate, NEG_INF), axis=1, keepdims=True)
        beats = (colval > gate) | ((colval == gate) & (bp < c_blk))
        rank = rank + jnp.where(beats, 1.0, 0.0)
    k_sel = min(MOBA_TOPK, n_past_blocks + 1)
    sel = jnp.where(same_head & (rank < k_sel), 1.0, 0.0)

    qb = q.astype(BF16)
    p_head = lax.broadcasted_iota(jnp.int32, (rows, page_rows), 0) % N_HEADS
    p_lane_head = lax.broadcasted_iota(jnp.int32, (rows, page_rows), 1) % N_HEADS
    page_same_head = p_head == p_lane_head
    for blk in range(n_past_blocks):
        blk_sel = jnp.max(jnp.where(c_blk == blk, sel, 0.0), axis=1, keepdims=True) > 0.5
        keep = page_same_head & blk_sel
        for p in range(blk * pages_per_block, (blk + 1) * pages_per_block):
            kpage = kp[p][...].reshape(page_rows, HEAD_DIM).astype(BF16)
            s = _nt_dot(qb, kpage) * scale
            s_ref[:, p * page_rows:(p + 1) * page_rows] = jnp.where(keep, s, NEG_INF)
    pad = jnp.zeros((LANES - rows, HEAD_DIM), F32)
    kn = jnp.concatenate([kn_ref[...], pad], axis=0).astype(BF16)
    vn = jnp.concatenate([vn_ref[...], pad], axis=0).astype(BF16)
    sn = _nt_dot(qb, kn) * scale
    n_row = lax.broadcasted_iota(jnp.int32, sn.shape, 0)
    n_col = lax.broadcasted_iota(jnp.int32, sn.shape, 1)
    keep_new = ((n_col % N_HEADS) == (n_row % N_HEADS)) & (n_col // N_HEADS <= n_row // N_HEADS) \
        & (n_col < rows)
    past = n_pages * page_rows
    s_ref[:, past:past + LANES] = jnp.where(keep_new, sn, NEG_INF)

    m = jnp.max(s_ref[...], axis=1, keepdims=True)
    l = jnp.zeros((rows, 1), F32)
    acc = jnp.zeros((rows, HEAD_DIM), F32)
    for p in range(n_pages):
        pp = jnp.exp(s_ref[:, p * page_rows:(p + 1) * page_rows] - m)
        l = l + jnp.sum(pp, axis=1, keepdims=True)
        vpage = vp[p][...].reshape(page_rows, HEAD_DIM).astype(BF16)
        acc = acc + jnp.dot(pp.astype(BF16), vpage, preferred_element_type=F32)
    pn = jnp.exp(s_ref[:, past:past + LANES] - m)
    l = l + jnp.sum(pn, axis=1, keepdims=True)
    acc = acc + jnp.dot(pn.astype(BF16), vn, preferred_element_type=F32)
    o_ref[...] = (acc / l).astype(o_ref.dtype)


def _moba_sample(q8, kn8, vn8, cache_k, cache_v, page_table, layer, *, n_seq, n_new, past_len):
    page_size = cache_k.shape[2]
    n_pages = page_table.shape[1]
    assert past_len == n_pages * page_size and past_len % MOBA_BLOCK == 0
    assert MOBA_BLOCK % page_size == 0 and n_new <= MOBA_BLOCK and n_new * N_HEADS <= LANES
    assert cache_k.shape[3:] == (N_HEADS, HEAD_DIM) and N_HEADS == SUBLANES
    rows = n_new * N_HEADS
    row_spec = pl.BlockSpec((rows, HEAD_DIM), lambda s, pt: (s, 0))

    def page_spec(p):
        return pl.BlockSpec((None, None, page_size, N_HEADS, HEAD_DIM),
                            lambda s, pt: (layer, pt[s, p], 0, 0, 0))

    page_specs = [page_spec(p) for p in range(n_pages)]
    grid_spec = pltpu.PrefetchScalarGridSpec(
        num_scalar_prefetch=1,
        grid=(n_seq,),
        in_specs=[row_spec, row_spec, row_spec] + page_specs + page_specs,
        out_specs=row_spec,
        scratch_shapes=[pltpu.VMEM((rows, n_pages * page_size * N_HEADS + LANES), F32)],
    )
    return pl.pallas_call(
        functools.partial(_moba_sample_body, n_pages=n_pages, page_size=page_size, n_new=n_new),
        grid_spec=grid_spec,
        out_shape=jax.ShapeDtypeStruct(q8.shape, BF16),
        compiler_params=_params(1),
        name="moba_sample",
    )(page_table, q8, kn8, vn8, *([cache_k] * n_pages), *([cache_v] * n_pages))


def _memkv_body(mem_ref, mn_ref, w_ref, mkn_ref, k_ref, v_ref):
    xn = _rms(mem_ref[...], mn_ref[...]).astype(BF16)
    kv = jnp.dot(xn, w_ref[...], preferred_element_type=F32)
    mkn = mkn_ref[...]
    for h in range(MEM_HEADS):
        sl = slice(h * MEM_HEAD_DIM, (h + 1) * MEM_HEAD_DIM)
        k_ref[:, sl] = _rms(kv[:, sl], mkn)
    v_ref[...] = kv[:, D_MEM:]


def _memkv(mem2, mem_norm, w_mem_kv, mk_norm):
    n_rows = mem2.shape[0]
    tm = N_MEM
    spec = pl.BlockSpec((tm, D_MEM), lambda i: (i, 0))
    return pl.pallas_call(
        _memkv_body,
        grid=(n_rows // tm,),
        in_specs=[pl.BlockSpec((tm, D_MODEL), lambda i: (i, 0)), _resident((1, D_MODEL)),
                  _resident((D_MODEL, 2 * D_MEM)), _resident((1, MEM_HEAD_DIM))],
        out_specs=(spec, spec),
        out_shape=(jax.ShapeDtypeStruct((n_rows, D_MEM), F32),) * 2,
        compiler_params=_params(1),
        name="memkv",
    )(mem2, mem_norm, w_mem_kv, mk_norm)


def _memattn_body(mq_ref, k_ref, v_ref, o_ref, *, groups, rows):
    scale = MEM_HEAD_DIM ** -0.5
    for g in range(groups):
        rs = slice(g * rows, (g + 1) * rows)
        for h in range(MEM_HEADS):
            sl = slice(h * MEM_HEAD_DIM, (h + 1) * MEM_HEAD_DIM)
            qh = mq_ref[rs, sl].astype(BF16)
            s = _nt_dot(qh, k_ref[g, :, sl].astype(BF16)) * scale
            p = jnp.exp(s - jnp.max(s, axis=1, keepdims=True))
            l = jnp.sum(p, axis=1, keepdims=True)
            o = jnp.dot(p.astype(BF16), v_ref[g, :, sl].astype(BF16), preferred_element_type=F32)
            o_ref[rs, sl] = (o / l).astype(o_ref.dtype)


def _memattn(mq2, mem_k3, mem_v3, *, groups, rows, steps_per_mem):
    n_tok = mq2.shape[0]
    tm = groups * rows
    row_spec = pl.BlockSpec((tm, D_MEM), lambda i: (i, 0))
    mem_spec = pl.BlockSpec((groups, N_MEM, D_MEM), lambda i: (i // steps_per_mem, 0, 0))
    return pl.pallas_call(
        functools.partial(_memattn_body, groups=groups, rows=rows),
        grid=(n_tok // tm,),
        in_specs=[row_spec, mem_spec, mem_spec],
        out_specs=row_spec,
        out_shape=jax.ShapeDtypeStruct((n_tok, D_MEM), BF16),
        compiler_params=_params(1),
        name="memattn",
    )(mq2, mem_k3, mem_v3)


def _post_body(x_ref, cy_ref, o_ref, om_ref, gate_ref, wco_ref, wao_ref, wmo_ref, wo_ref,
               fn_ref, wq_ref, h_ref, xn_ref, qh_ref):
    conv_out = jnp.dot(cy_ref[...], wco_ref[...], preferred_element_type=F32)
    attn_out = jnp.dot(o_ref[...], wao_ref[...], preferred_element_type=F32)
    mem_out = jnp.dot(om_ref[...], wmo_ref[...], preferred_element_type=F32)
    merged = (jax.nn.sigmoid(gate_ref[:, 0:D_MODEL]) * conv_out
              + jax.nn.sigmoid(gate_ref[:, D_MODEL:2 * D_MODEL]) * attn_out
              + jax.nn.sigmoid(gate_ref[:, 2 * D_MODEL:3 * D_MODEL]) * mem_out)
    h = x_ref[...] + jnp.dot(merged.astype(BF16), wo_ref[...], preferred_element_type=F32)
    h_ref[...] = h
    xn = _rms(h, fn_ref[...]).astype(BF16)
    xn_ref[...] = xn
    qh = jnp.dot(xn, wq_ref[...], preferred_element_type=F32)
    for hc in range(2 * PEER_HEADS):
        qh_ref[hc] = qh[:, hc * PEER_DKEY:(hc + 1) * PEER_DKEY]


def _post(x2, cy, o2, om, gates, wco, wao, wmo, wo, ffn_norm, wq):
    n_tok = x2.shape[0]
    tm = min(TOKEN_TILE, n_tok)

    def rows(width):
        return pl.BlockSpec((tm, width), lambda i: (i, 0))

    return pl.pallas_call(
        _post_body,
        grid=(n_tok // tm,),
        in_specs=[rows(D_MODEL), rows(D_CONV), rows(D_ATTN), rows(D_MEM), rows(3 * D_MODEL),
                  _resident(wco.shape), _resident(wao.shape), _resident(wmo.shape),
                  _resident(wo.shape), _resident((1, D_MODEL)), _resident(wq.shape)],
        out_specs=(rows(D_MODEL), rows(D_MODEL),
                   pl.BlockSpec((2 * PEER_HEADS, tm, PEER_DKEY), lambda i: (0, i, 0))),
        out_shape=(jax.ShapeDtypeStruct((n_tok, D_MODEL), F32),
                   jax.ShapeDtypeStruct((n_tok, D_MODEL), BF16),
                   jax.ShapeDtypeStruct((2 * PEER_HEADS, n_tok, PEER_DKEY), F32)),
        compiler_params=_params(1),
        name="post",
    )(x2, cy, o2, om, gates, wco, wao, wmo, wo, ffn_norm, wq)


_CAND_PIECES = (
    ((0, 1), (0, 16), 16), ((1, 2), (0, 8), 8), ((2, 3), (0, 8), 8), ((3, 4), (0, 8), 8),
    ((8, 16), (0, 1), 8), ((0, 8), (0, 1), 8), ((0, 8), (1, 2), 8), ((0, 8), (2, 3), 8))


def _route_body(qh_ref, sub_ref, i_ref, j_ref, g_ref,
                st1_ref, ix1_ref, st2_ref, ix2_ref, best_ref, si_ref, sj_ref, sg_ref):
    tm = qh_ref.shape[1]
    k = PEER_TOPK
    big = 1e9
    key_iota = lax.broadcasted_iota(jnp.int32, (PEER_NKEYS, tm), 0).astype(F32)

    def top_keys(hc, st_ref, ix_ref):
        scores = _nt_dot(sub_ref[hc], qh_ref[hc], precision=HIGHEST)

        def rnd(r, x):
            m = jnp.max(x, axis=0, keepdims=True)
            f = jnp.min(jnp.where(x == m, key_iota, big), axis=0, keepdims=True)
            st_ref[pl.ds(r, 1), :] = m
            ix_ref[pl.ds(r, 1), :] = f
            return jnp.where(key_iota == f, NEG_INF, x)

        lax.fori_loop(0, k, rnd, scores)

    r8 = lax.broadcasted_iota(jnp.int32, (SUBLANES, tm), 0).astype(F32)
    r16 = lax.broadcasted_iota(jnp.int32, (2 * SUBLANES, tm), 0).astype(F32)
    fidx = jnp.concatenate([
        r16, k + r8, 2 * k + r8, 3 * k + r8,
        (r8 + SUBLANES) * k, r8 * k, r8 * k + 1, r8 * k + 2], axis=0)
    valid = jnp.concatenate([
        r16 >= 0, r8 >= 0, r8 < 5, r8 < 4,
        r8 >= 0, r8 >= 4, r8 >= 4, r8 == 4], axis=0)
    fidx = jnp.where(valid, fidx, -1.0)

    def head(h, carry):
        top_keys(2 * h, st1_ref, ix1_ref)
        top_keys(2 * h + 1, st2_ref, ix2_ref)

        def first(ref):
            return jnp.concatenate(
                [jnp.broadcast_to(ref[a0:a1, :], (n, tm)) for (a0, a1), _, n in _CAND_PIECES], axis=0)

        def second(ref):
            return jnp.concatenate(
                [jnp.broadcast_to(ref[b0:b1, :], (n, tm)) for _, (b0, b1), n in _CAND_PIECES], axis=0)

        cand = jnp.where(valid, first(st1_ref) + second(st2_ref), NEG_INF)
        ci = first(ix1_ref)
        cj = second(ix2_ref)

        def rnd(r, x):
            m = jnp.max(x, axis=0, keepdims=True)
            f = jnp.min(jnp.where(x == m, fidx, big), axis=0, keepdims=True)
            hit = fidx == f
            best_ref[pl.ds(r, 1), :] = m
            row = pl.multiple_of(h * k, k) + r
            si_ref[pl.ds(row, 1), :] = jnp.sum(jnp.where(hit, ci, 0.0), axis=0, keepdims=True)
            sj_ref[pl.ds(row, 1), :] = jnp.sum(jnp.where(hit, cj, 0.0), axis=0, keepdims=True)
            return jnp.where(hit, NEG_INF, x)

        lax.fori_loop(0, k, rnd, cand)
        best = best_ref[...]
        e = jnp.exp(best - best[0:1, :])
        sg_ref[pl.ds(pl.multiple_of(h * k, k), k), :] = e / jnp.sum(e, axis=0, keepdims=True)
        return carry

    lax.fori_loop(0, PEER_HEADS, head, 0)
    i_ref[...] = si_ref[...].T
    j_ref[...] = sj_ref[...].T
    g_ref[...] = sg_ref[...].T


def _route(qh3, subkeys):
    n_tok = qh3.shape[1]
    tm = min(ROUTE_TILE, n_tok)
    n_slots = PEER_HEADS * PEER_TOPK
    assert n_slots == LANES
    sub3 = subkeys.reshape(2 * PEER_HEADS, PEER_NKEYS, PEER_DKEY)
    slot_spec = pl.BlockSpec((tm, n_slots), lambda i: (i, 0))
    small = pltpu.VMEM((PEER_TOPK, tm), F32)
    slots = pltpu.VMEM((n_slots, tm), F32)
    return pl.pallas_call(
        _route_body,
        grid=(n_tok // tm,),
        in_specs=[pl.BlockSpec((2 * PEER_HEADS, tm, PEER_DKEY), lambda i: (0, i, 0)),
                  _resident(sub3.shape)],
        out_specs=(slot_spec,) * 3,
        out_shape=(jax.ShapeDtypeStruct((n_tok, n_slots), F32),) * 3,
        scratch_shapes=[small, small, small, small, small, slots, slots, slots],
        compiler_params=_params(1),
        name="peer_route",
    )(qh3, sub3)


def _wbuild_body(i_ref, j_ref, g_ref, w_ref):
    tm = i_ref.shape[0]
    key = lax.broadcasted_iota(jnp.int32, (PEER_NKEYS, LANES), 0).astype(F32)

    def token(t, carry):
        irow = i_ref[pl.ds(t, 1), :]
        jrow = j_ref[pl.ds(t, 1), :]
        grow = g_ref[pl.ds(t, 1), :]
        a = jnp.where(key == irow, grow, 0.0).astype(BF16)
        b = jnp.where(key == jrow, 1.0, 0.0).astype(BF16)
        w_ref[t] = _nt_dot(a, b).astype(BF16)
        return carry

    lax.fori_loop(0, tm, token, 0)


def _wbuild(si, sj, sg):
    n_tok = si.shape[0]
    tm = min(WBUILD_TILE, n_tok)
    slot_spec = pl.BlockSpec((tm, LANES), lambda i: (i, 0))
    return pl.pallas_call(
        _wbuild_body,
        grid=(n_tok // tm,),
        in_specs=[slot_spec] * 3,
        out_specs=pl.BlockSpec((tm, PEER_NKEYS, PEER_NKEYS), lambda i: (i, 0, 0)),
        out_shape=jax.ShapeDtypeStruct((n_tok, PEER_NKEYS, PEER_NKEYS), BF16),
        compiler_params=_params(1),
        name="peer_wbuild",
    )(si, sj, sg)


def _peer_body(xn_ref, u_ref, v_ref, w_ref, h_ref, y_ref, acc_ref):
    c = pl.program_id(1)

    @pl.when(c == 0)
    def _():
        acc_ref[...] = jnp.zeros(acc_ref.shape, F32)

    xn = xn_ref[...]
    sub = PEER_EXPERT_SUB
    for s in range(u_ref.shape[0] // sub):
        sl = slice(s * sub, (s + 1) * sub)
        z = _nt_dot(xn, u_ref[sl, :])
        a = 0.5 * z * (1.0 + lax.erf(z * (2.0 ** -0.5)))
        a = a * w_ref[:, sl].astype(F32)
        acc_ref[...] += jnp.dot(a.astype(BF16), v_ref[sl, :], preferred_element_type=F32)

    @pl.when(c == pl.num_programs(1) - 1)
    def _():
        y_ref[...] = h_ref[...] + acc_ref[...]


def _peer_dense(xn, w2, h, peer_u, peer_v):
    n_tok = xn.shape[0]
    tm = min(PEER_TOKEN_TILE, n_tok)
    ec = PEER_EXPERT_CHUNK
    assert n_tok % tm == 0 and PEER_N_EXPERTS % ec == 0 and ec % PEER_EXPERT_SUB == 0
    tok_spec = pl.BlockSpec((tm, D_MODEL), lambda i, c: (i, 0))
    exp_spec = pl.BlockSpec((ec, D_MODEL), lambda i, c: (c, 0))
    return pl.pallas_call(
        _peer_body,
        grid=(n_tok // tm, PEER_N_EXPERTS // ec),
        in_specs=[tok_spec, exp_spec, exp_spec,
                  pl.BlockSpec((tm, ec), lambda i, c: (i, c)), tok_spec],
        out_specs=tok_spec,
        out_shape=jax.ShapeDtypeStruct((n_tok, D_MODEL), F32),
        scratch_shapes=[pltpu.VMEM((tm, D_MODEL), F32)],
        compiler_params=_params(2),
        name="peer_dense",
    )(xn, peer_u, peer_v, w2, h)


def _row(v):
    return v.reshape(1, -1)


def _channel_mixer(x2, cy, o2, om, gates, lw):
    h, xn, qh3 = _post(x2, cy, o2, om, gates, lw["wco"], lw["wao"], lw["wmo"], lw["wo"],
                       lw["ffn_norm"], lw["wq"])
    si, sj, sg = _route(qh3, lw["subkeys"])
    w3 = _wbuild(si, sj, sg)
    w2 = w3.reshape(w3.shape[0], PEER_N_EXPERTS)
    return _peer_dense(xn, w2, h, lw["peer_u"], lw["peer_v"])


def kernel(x_prompt, x_sample, mem_prompt, cache_k, cache_v, cache_conv, cache_mem_k, cache_mem_v,
           page_table, attn_norm, w_in, conv_w, w_conv_out, q_norm, k_norm, w_attn_out,
           mem_norm, w_mem_kv, mk_norm, mq_norm, w_mem_out, w_o,
           ffn_norm, peer_wq, peer_subkeys, peer_u, peer_v):
    n_p, t_p, _ = x_prompt.shape
    n_s, t_s, _ = x_sample.shape
    depth = attn_norm.shape[0]
    past_len = page_table.shape[1] * cache_k.shape[2]
    h_p = x_prompt.reshape(n_p * t_p, D_MODEL)
    h_s = x_sample.reshape(n_s * t_s, D_MODEL)
    outs = [[] for _ in range(8)]
    for l in range(depth):
        lw = dict(
            wco=w_conv_out[l].astype(BF16), wao=w_attn_out[l].astype(BF16),
            wmo=w_mem_out[l].astype(BF16), wo=w_o[l].astype(BF16), wq=peer_wq[l].astype(BF16),
            ffn_norm=_row(ffn_norm[l]), subkeys=peer_subkeys[l],
            peer_u=peer_u[l].astype(BF16), peer_v=peer_v[l].astype(BF16))
        in_w = (_row(attn_norm[l]), w_in[l].astype(BF16), _row(q_norm[l]), _row(k_norm[l]),
                _row(mq_norm[l]), conv_w[l])

        mem_k, mem_v = _memkv(mem_prompt.reshape(n_p * N_MEM, D_MODEL), _row(mem_norm[l]),
                              w_mem_kv[l].astype(BF16), _row(mk_norm[l]))
        zero_state = jnp.zeros((SUBLANES, D_CONV), F32)
        cy, u, q, k, v, mq, gates, kmean = _inproj(h_p, *in_w, zero_state, zero_state, seq_len=t_p)
        o = _moba_prompt(q, k, v, kmean, n_seq=n_p, seq_len=t_p)
        om = _memattn(mq, mem_k.reshape(n_p, N_MEM, D_MEM), mem_v.reshape(n_p, N_MEM, D_MEM),
                      groups=1, rows=TOKEN_TILE, steps_per_mem=t_p // TOKEN_TILE)
        h_p = _channel_mixer(h_p, cy, o, om, gates, lw)
        outs[0].append(k.reshape(n_p, t_p, N_HEADS, HEAD_DIM))
        outs[1].append(v.reshape(n_p, t_p, N_HEADS, HEAD_DIM))
        outs[2].append(u.reshape(n_p, t_p, D_CONV)[:, t_p - (CONV_W - 1):])
        outs[3].append(mem_k.reshape(n_p, N_MEM, MEM_HEADS, MEM_HEAD_DIM))
        outs[4].append(mem_v.reshape(n_p, N_MEM, MEM_HEADS, MEM_HEAD_DIM))

        state = cache_conv[l]
        s0x = jnp.repeat(state[:, 0], t_s, axis=0)
        s1x = jnp.repeat(state[:, 1], t_s, axis=0)
        cy, u, q, k, v, mq, gates, _ = _inproj(h_s, *in_w, s0x, s1x, seq_len=t_s)
        k4 = k.reshape(n_s, t_s, N_HEADS, HEAD_DIM)
        v4 = v.reshape(n_s, t_s, N_HEADS, HEAD_DIM)
        rows = n_s * t_s * N_HEADS
        o = _moba_sample(q.reshape(rows, HEAD_DIM), k4.reshape(rows, HEAD_DIM),
                         v4.reshape(rows, HEAD_DIM), cache_k, cache_v, page_table, l,
                         n_seq=n_s, n_new=t_s, past_len=past_len)
        o = o.reshape(n_s * t_s, D_ATTN)
        groups = SUBLANES
        om = _memattn(mq, cache_mem_k[l].reshape(n_s, N_MEM, D_MEM),
                      cache_mem_v[l].reshape(n_s, N_MEM, D_MEM),
                      groups=groups, rows=t_s, steps_per_mem=1)
        h_s = _channel_mixer(h_s, cy, o, om, gates, lw)
        outs[5].append(k4)
        outs[6].append(v4)
        ext = jnp.concatenate([state, u.reshape(n_s, t_s, D_CONV)], axis=1)
        outs[7].append(ext[:, t_s:])
    stacked = [jnp.stack(o) for o in outs]
    return (h_p.reshape(n_p, t_p, D_MODEL), h_s.reshape(n_s, t_s, D_MODEL), *stacked)
```

```python
import functools

import jax
import jax.numpy as jnp
from jax import lax
from jax.experimental import pallas as pl
from jax.experimental.pallas import tpu as pltpu

F32 = jnp.float32
BF16 = jnp.bfloat16
HIGHEST = lax.Precision.HIGHEST
NEG_INF = float("-inf")

SUBLANES = 8
LANES = 128
VMEM_LIMIT_BYTES = 56 * 1024 * 1024

EPS = 1e-6
D_MODEL = 1024
D_CONV = 512
CONV_W = 3
N_HEADS = 8
HEAD_DIM = 128
D_ATTN = N_HEADS * HEAD_DIM
MOBA_BLOCK = 256
MOBA_TOPK = 3
N_MEM = 256
MEM_HEADS = 4
MEM_HEAD_DIM = 128
D_MEM = MEM_HEADS * MEM_HEAD_DIM
PEER_HEADS = 8
PEER_NKEYS = 128
PEER_DKEY = 128
PEER_TOPK = 16
PEER_N_EXPERTS = PEER_NKEYS * PEER_NKEYS
D_PEER_Q = PEER_HEADS * 2 * PEER_DKEY
D_IN = 3 * D_CONV + 3 * D_ATTN + D_MEM + 3 * D_MODEL
OFF_CB, OFF_CC, OFF_CH = 0, D_CONV, 2 * D_CONV
OFF_Q = 3 * D_CONV
OFF_K = OFF_Q + D_ATTN
OFF_V = OFF_K + D_ATTN
OFF_MQ = OFF_V + D_ATTN
OFF_G = OFF_MQ + D_MEM

TOKEN_TILE = 256
ROUTE_TILE = 256
WBUILD_TILE = 128
PEER_TOKEN_TILE = 512
PEER_EXPERT_CHUNK = 2048
PEER_EXPERT_SUB = 512


def _nt_dot(a, b, precision=None):
    return lax.dot_general(a, b, (((1,), (1,)), ((), ())), precision=precision,
                           preferred_element_type=F32)


def _params(n_grid_dims):
    return pltpu.CompilerParams(dimension_semantics=("arbitrary",) * n_grid_dims,
                                vmem_limit_bytes=VMEM_LIMIT_BYTES)


def _rms(x, gain):
    r = lax.rsqrt(jnp.mean(x * x, axis=-1, keepdims=True) + EPS)
    return x * r * gain


def _resident(shape):
    nd = len(shape)
    return pl.BlockSpec(shape, lambda *_: (0,) * nd, pipeline_mode=pl.Buffered(1))


def _inproj_body(x_ref, an_ref, w_ref, qn_ref, kn_ref, mqn_ref, cw_ref, s0_ref, s1_ref,
                 cy_ref, u_ref, q_ref, k_ref, v_ref, mq_ref, gate_ref, kmean_ref,
                 ubuf_ref, *, tiles_per_seq, seq_len):
    tm = x_ref.shape[0]
    xn = _rms(x_ref[...], an_ref[...]).astype(BF16)

    def proj(off, width):
        return jnp.dot(xn, w_ref[:, off:off + width], preferred_element_type=F32)

    cb = proj(OFF_CB, D_CONV)
    u = proj(OFF_CC, D_CONV) * proj(OFF_CH, D_CONV)
    u_ref[...] = u
    if seq_len >= tm:
        @pl.when(pl.program_id(0) % tiles_per_seq == 0)
        def _():
            ubuf_ref[0:SUBLANES, :] = jnp.zeros((SUBLANES, D_CONV), F32)
    else:
        ubuf_ref[0:SUBLANES, :] = jnp.zeros((SUBLANES, D_CONV), F32)
    ubuf_ref[SUBLANES:SUBLANES + tm, :] = u
    um1 = ubuf_ref[SUBLANES - 1:SUBLANES - 1 + tm, :]
    um2 = ubuf_ref[SUBLANES - 2:SUBLANES - 2 + tm, :]
    if seq_len >= tm:
        ubuf_ref[0:SUBLANES, :] = ubuf_ref[tm:tm + SUBLANES, :]
    else:
        assert seq_len == SUBLANES
        tl = lax.broadcasted_iota(jnp.int32, (tm, D_CONV), 0) % seq_len
        s0 = s0_ref[...]
        s1 = s1_ref[...]
        um1 = jnp.where(tl == 0, s1, um1)
        um2 = jnp.where(tl == 0, s0, jnp.where(tl == 1, s1, um2))
    cw = cw_ref[...]
    y = cw[0:1, :] * um2 + cw[1:2, :] * um1 + cw[2:3, :] * u
    cy_ref[...] = (cb * y).astype(BF16)

    qf = proj(OFF_Q, D_ATTN)
    kf = proj(OFF_K, D_ATTN)
    qn = qn_ref[...]
    kn = kn_ref[...]
    for h in range(N_HEADS):
        sl = slice(h * HEAD_DIM, (h + 1) * HEAD_DIM)
        q_ref[:, sl] = _rms(qf[:, sl], qn)
        k_ref[:, sl] = _rms(kf[:, sl], kn)
    kmean_ref[0] = jnp.mean(k_ref[...], axis=0, keepdims=True)
    v_ref[...] = proj(OFF_V, D_ATTN)
    mqf = proj(OFF_MQ, D_MEM)
    mqn = mqn_ref[...]
    for h in range(MEM_HEADS):
        sl = slice(h * MEM_HEAD_DIM, (h + 1) * MEM_HEAD_DIM)
        mq_ref[:, sl] = _rms(mqf[:, sl], mqn)
    gate_ref[...] = proj(OFF_G, 3 * D_MODEL)


def _inproj(x2, attn_norm, w_in, q_norm, k_norm, mq_norm, conv_w, s0x, s1x, *, seq_len):
    n_tok = x2.shape[0]
    tm = min(TOKEN_TILE, n_tok)
    assert n_tok % tm == 0 and (seq_len % tm == 0 or seq_len == SUBLANES)
    n_tiles = n_tok // tm
    tiles_per_seq = max(seq_len // tm, 1)
    long_seq = seq_len >= tm

    def rows(width):
        return pl.BlockSpec((tm, width), lambda i: (i, 0))

    state_spec = _resident(s0x.shape) if long_seq else rows(D_CONV)
    out_shape = (
        jax.ShapeDtypeStruct((n_tok, D_CONV), BF16),
        jax.ShapeDtypeStruct((n_tok, D_CONV), F32),
        jax.ShapeDtypeStruct((n_tok, D_ATTN), F32),
        jax.ShapeDtypeStruct((n_tok, D_ATTN), F32),
        jax.ShapeDtypeStruct((n_tok, D_ATTN), F32),
        jax.ShapeDtypeStruct((n_tok, D_MEM), F32),
        jax.ShapeDtypeStruct((n_tok, 3 * D_MODEL), F32),
        jax.ShapeDtypeStruct((n_tiles, 1, D_ATTN), F32),
    )
    return pl.pallas_call(
        functools.partial(_inproj_body, tiles_per_seq=tiles_per_seq, seq_len=seq_len),
        grid=(n_tiles,),
        in_specs=[rows(D_MODEL), _resident((1, D_MODEL)), _resident((D_MODEL, D_IN)),
                  _resident((1, HEAD_DIM)), _resident((1, HEAD_DIM)), _resident((1, MEM_HEAD_DIM)),
                  _resident((CONV_W, D_CONV)), state_spec, state_spec],
        out_specs=(rows(D_CONV), rows(D_CONV), rows(D_ATTN), rows(D_ATTN), rows(D_ATTN),
                   rows(D_MEM), rows(3 * D_MODEL),
                   pl.BlockSpec((1, 1, D_ATTN), lambda i: (i, 0, 0))),
        out_shape=out_shape,
        scratch_shapes=[pltpu.VMEM((tm + 2 * SUBLANES, D_CONV), F32)],
        compiler_params=_params(1),
        name="inproj",
    )(x2, attn_norm, w_in, q_norm, k_norm, mq_norm, conv_w, s0x, s1x)


def _moba_prompt_body(q_ref, k_ref, v_ref, km_ref, o_ref, kb_ref, vt_ref, sel_ref, *, n_blocks):
    b = pl.program_id(2)
    blk = MOBA_BLOCK
    scale = HEAD_DIM ** -0.5

    @pl.when(b == 0)
    def _():
        kb_ref[...] = k_ref[...].astype(BF16)
        for j in range(n_blocks):
            vt_ref[:, j * blk:(j + 1) * blk] = v_ref[j * blk:(j + 1) * blk, :].T.astype(BF16)

    q = q_ref[...]
    gate = _nt_dot(km_ref[0], q, precision=HIGHEST)
    blk_id = lax.broadcasted_iota(jnp.int32, gate.shape, 0)
    rank = jnp.zeros(gate.shape, F32)
    for jp in range(n_blocks - 1):
        other = gate[jp:jp + 1, :]
        beats = (other > gate) | ((other == gate) & (jp < blk_id))
        rank = rank + jnp.where(beats, jnp.where(jp < b, 1.0, 0.0), 0.0)
    k_sel = min(MOBA_TOPK, n_blocks)
    sel_ref[...] = jnp.where((rank < k_sel) & (blk_id < b), 1.0, 0.0)

    qt = q.T.astype(BF16)
    key_row = lax.broadcasted_iota(jnp.int32, (blk, blk), 0)
    query_col = lax.broadcasted_iota(jnp.int32, (blk, blk), 1)

    for own in range(n_blocks):
        @pl.when(b == own)
        def _(own=own):
            scores = []
            m = None
            for j in range(own + 1):
                s = jnp.dot(kb_ref[j * blk:(j + 1) * blk, :], qt, preferred_element_type=F32) * scale
                if j == own:
                    s = jnp.where(key_row <= query_col, s, NEG_INF)
                else:
                    s = jnp.where(sel_ref[j:j + 1, :] > 0.5, s, NEG_INF)
                scores.append(s)
                mj = jnp.max(s, axis=0, keepdims=True)
                m = mj if m is None else jnp.maximum(m, mj)
            l = jnp.zeros((1, blk), F32)
            acc = jnp.zeros((HEAD_DIM, blk), F32)
            for j in range(own + 1):
                p = jnp.exp(scores[j] - m)
                l = l + jnp.sum(p, axis=0, keepdims=True)
                acc = acc + jnp.dot(vt_ref[:, j * blk:(j + 1) * blk], p.astype(BF16),
                                    preferred_element_type=F32)
            o_ref[...] = (acc / l).T.astype(o_ref.dtype)


def _moba_prompt(q2, k2, v2, kmean, *, n_seq, seq_len):
    blk = MOBA_BLOCK
    assert seq_len % blk == 0 and TOKEN_TILE == blk
    nb = seq_len // blk
    km = kmean.reshape(n_seq, nb, D_ATTN)
    qo_spec = pl.BlockSpec((blk, HEAD_DIM), lambda n, h, b: (n * nb + b, h))
    kv_spec = pl.BlockSpec((seq_len, HEAD_DIM), lambda n, h, b: (n, h))
    return pl.pallas_call(
        functools.partial(_moba_prompt_body, n_blocks=nb),
        grid=(n_seq, N_HEADS, nb),
        in_specs=[qo_spec, kv_spec, kv_spec,
                  pl.BlockSpec((1, nb, HEAD_DIM), lambda n, h, b: (n, 0, h))],
        out_specs=qo_spec,
        out_shape=jax.ShapeDtypeStruct(q2.shape, BF16),
        scratch_shapes=[pltpu.VMEM((seq_len, HEAD_DIM), BF16), pltpu.VMEM((HEAD_DIM, seq_len), BF16),
                        pltpu.VMEM((nb, blk), F32)],
        compiler_params=_params(3),
        name="moba_prompt",
    )(q2, k2, v2, km)


def _moba_sample_body(pt_ref, q_ref, kn_ref, vn_ref, *rest, n_pages, page_size, n_new):
    del pt_ref
    kp = rest[:n_pages]
    vp = rest[n_pages:2 * n_pages]
    o_ref, s_ref = rest[2 * n_pages:]
    rows = n_new * N_HEADS
    page_rows = page_size * N_HEADS
    pages_per_block = MOBA_BLOCK // page_size
    n_past_blocks = n_pages // pages_per_block
    scale = HEAD_DIM ** -0.5

    q = q_ref[...]
    kmeans = []
    for blk in range(n_past_blocks):
        acc = jnp.zeros((N_HEADS, HEAD_DIM), F32)
        for p in range(blk * pages_per_block, (blk + 1) * pages_per_block):
            acc = acc + jnp.sum(kp[p][...], axis=0)
        kmeans.append(acc * (1.0 / MOBA_BLOCK))
    kmean = jnp.concatenate(kmeans, axis=0)
    gate = _nt_dot(q, kmean, precision=HIGHEST)
    r_head = lax.broadcasted_iota(jnp.int32, gate.shape, 0) % N_HEADS
    c_idx = lax.broadcasted_iota(jnp.int32, gate.shape, 1)
    c_blk = c_idx // N_HEADS
    same_head = (c_idx % N_HEADS) == r_head
    rank = jnp.zeros(gate.shape, F32)
    for bp in range(n_past_blocks):
        colval = jnp.max(jnp.where(same_head & (c_blk == bp), gate, NEG_INF), axis=1, keepdims=True)
        beats = (colval > gate) | ((colval == gate) & (bp < c_blk))
        rank = rank + jnp.where(beats, 1.0, 0.0)
    k_sel = min(MOBA_TOPK, n_past_blocks + 1)
    sel = jnp.where(same_head & (rank < k_sel), 1.0, 0.0)

    qb = q.astype(BF16)
    p_head = lax.broadcasted_iota(jnp.int32, (rows, page_rows), 0) % N_HEADS
    p_lane_head = lax.broadcasted_iota(jnp.int32, (rows, page_rows), 1) % N_HEADS
    page_same_head = p_head == p_lane_head
    for blk in range(n_past_blocks):
        blk_sel = jnp.max(jnp.where(c_blk == blk, sel, 0.0), axis=1, keepdims=True) > 0.5
        keep = page_same_head & blk_sel
        for p in range(blk * pages_per_block, (blk + 1) * pages_per_block):
            kpage = kp[p][...].reshape(page_rows, HEAD_DIM).astype(BF16)
            s = _nt_dot(qb, kpage) * scale
            s_ref[:, p * page_rows:(p + 1) * page_rows] = jnp.where(keep, s, NEG_INF)
    pad = jnp.zeros((LANES - rows, HEAD_DIM), F32)
    kn = jnp.concatenate([kn_ref[...], pad], axis=0).astype(BF16)
    vn = jnp.concatenate([vn_ref[...], pad], axis=0).astype(BF16)
    sn = _nt_dot(qb, kn) * scale
    n_row = lax.broadcasted_iota(jnp.int32, sn.shape, 0)
    n_col = lax.broadcasted_iota(jnp.int32, sn.shape, 1)
    keep_new = ((n_col % N_HEADS) == (n_row % N_HEADS)) & (n_col // N_HEADS <= n_row // N_HEADS) \
        & (n_col < rows)
    past = n_pages * page_rows
    s_ref[:, past:past + LANES] = jnp.where(keep_new, sn, NEG_INF)

    m = jnp.max(s_ref[...], axis=1, keepdims=True)
    l = jnp.zeros((rows, 1), F32)
    acc = jnp.zeros((rows, HEAD_DIM), F32)
    for p in range(n_pages):
        pp = jnp.exp(s_ref[:, p * page_rows:(p + 1) * page_rows] - m)
        l = l + jnp.sum(pp, axis=1, keepdims=True)
        vpage = vp[p][...].reshape(page_rows, HEAD_DIM).astype(BF16)
        acc = acc + jnp.dot(pp.astype(BF16), vpage, preferred_element_type=F32)
    pn = jnp.exp(s_ref[:, past:past + LANES] - m)
    l = l + jnp.sum(pn, axis=1, keepdims=True)
    acc = acc + jnp.dot(pn.astype(BF16), vn, preferred_element_type=F32)
    o_ref[...] = (acc / l).astype(o_ref.dtype)


def _moba_sample(q8, kn8, vn8, cache_k, cache_v, page_table, layer, *, n_seq, n_new, past_len):
    page_size = cache_k.shape[2]
    n_pages = page_table.shape[1]
    assert past_len == n_pages * page_size and past_len % MOBA_BLOCK == 0
    assert MOBA_BLOCK % page_size == 0 and n_new <= MOBA_BLOCK and n_new * N_HEADS <= LANES
    assert cache_k.shape[3:] == (N_HEADS, HEAD_DIM) and N_HEADS == SUBLANES
    rows = n_new * N_HEADS
    row_spec = pl.BlockSpec((rows, HEAD_DIM), lambda s, pt: (s, 0))

    def page_spec(p):
        return pl.BlockSpec((None, None, page_size, N_HEADS, HEAD_DIM),
                            lambda s, pt: (layer, pt[s, p], 0, 0, 0))

    page_specs = [page_spec(p) for p in range(n_pages)]
    grid_spec = pltpu.PrefetchScalarGridSpec(
        num_scalar_prefetch=1,
        grid=(n_seq,),
        in_specs=[row_spec, row_spec, row_spec] + page_specs + page_specs,
        out_specs=row_spec,
        scratch_shapes=[pltpu.VMEM((rows, n_pages * page_size * N_HEADS + LANES), F32)],
    )
    return pl.pallas_call(
        functools.partial(_moba_sample_body, n_pages=n_pages, page_size=page_size, n_new=n_new),
        grid_spec=grid_spec,
        out_shape=jax.ShapeDtypeStruct(q8.shape, BF16),
        compiler_params=_params(1),
        name="moba_sample",
    )(page_table, q8, kn8, vn8, *([cache_k] * n_pages), *([cache_v] * n_pages))


def _memkv_body(mem_ref, mn_ref, w_ref, mkn_ref, k_ref, v_ref):
    xn = _rms(mem_ref[...], mn_ref[...]).astype(BF16)
    kv = jnp.dot(xn, w_ref[...], preferred_element_type=F32)
    mkn = mkn_ref[...]
    for h in range(MEM_HEADS):
        sl = slice(h * MEM_HEAD_DIM, (h + 1) * MEM_HEAD_DIM)
        k_ref[:, sl] = _rms(kv[:, sl], mkn)
    v_ref[...] = kv[:, D_MEM:]


def _memkv(mem2, mem_norm, w_mem_kv, mk_norm):
    n_rows = mem2.shape[0]
    tm = N_MEM
    spec = pl.BlockSpec((tm, D_MEM), lambda i: (i, 0))
    return pl.pallas_call(
        _memkv_body,
        grid=(n_rows // tm,),
        in_specs=[pl.BlockSpec((tm, D_MODEL), lambda i: (i, 0)), _resident((1, D_MODEL)),
                  _resident((D_MODEL, 2 * D_MEM)), _resident((1, MEM_HEAD_DIM))],
        out_specs=(spec, spec),
        out_shape=(jax.ShapeDtypeStruct((n_rows, D_MEM), F32),) * 2,
        compiler_params=_params(1),
        name="memkv",
    )(mem2, mem_norm, w_mem_kv, mk_norm)


def _memattn_body(mq_ref, k_ref, v_ref, o_ref, *, groups, rows):
    scale = MEM_HEAD_DIM ** -0.5
    for g in range(groups):
        rs = slice(g * rows, (g + 1) * rows)
        for h in range(MEM_HEADS):
            sl = slice(h * MEM_HEAD_DIM, (h + 1) * MEM_HEAD_DIM)
            qh = mq_ref[rs, sl].astype(BF16)
            s = _nt_dot(qh, k_ref[g, :, sl].astype(BF16)) * scale
            p = jnp.exp(s - jnp.max(s, axis=1, keepdims=True))
            l = jnp.sum(p, axis=1, keepdims=True)
            o = jnp.dot(p.astype(BF16), v_ref[g, :, sl].astype(BF16), preferred_element_type=F32)
            o_ref[rs, sl] = (o / l).astype(o_ref.dtype)


def _memattn(mq2, mem_k3, mem_v3, *, groups, rows, steps_per_mem):
    n_tok = mq2.shape[0]
    tm = groups * rows
    row_spec = pl.BlockSpec((tm, D_MEM), lambda i: (i, 0))
    mem_spec = pl.BlockSpec((groups, N_MEM, D_MEM), lambda i: (i // steps_per_mem, 0, 0))
    return pl.pallas_call(
        functools.partial(_memattn_body, groups=groups, rows=rows),
        grid=(n_tok // tm,),
        in_specs=[row_spec, mem_spec, mem_spec],
        out_specs=row_spec,
        out_shape=jax.ShapeDtypeStruct((n_tok, D_MEM), BF16),
        compiler_params=_params(1),
        name="memattn",
    )(mq2, mem_k3, mem_v3)


def _post_body(x_ref, cy_ref, o_ref, om_ref, gate_ref, wco_ref, wao_ref, wmo_ref, wo_ref,
               fn_ref, wq_ref, h_ref, xn_ref, qh_ref):
    conv_out = jnp.dot(cy_ref[...], wco_ref[...], preferred_element_type=F32)
    attn_out = jnp.dot(o_ref[...], wao_ref[...], preferred_element_type=F32)
    mem_out = jnp.dot(om_ref[...], wmo_ref[...], preferred_element_type=F32)
    merged = (jax.nn.sigmoid(gate_ref[:, 0:D_MODEL]) * conv_out
              + jax.nn.sigmoid(gate_ref[:, D_MODEL:2 * D_MODEL]) * attn_out
              + jax.nn.sigmoid(gate_ref[:, 2 * D_MODEL:3 * D_MODEL]) * mem_out)
    h = x_ref[...] + jnp.dot(merged.astype(BF16), wo_ref[...], preferred_element_type=F32)
    h_ref[...] = h
    xn = _rms(h, fn_ref[...]).astype(BF16)
    xn_ref[...] = xn
    qh = jnp.dot(xn, wq_ref[...], preferred_element_type=F32)
    for hc in range(2 * PEER_HEADS):
        qh_ref[hc] = qh[:, hc * PEER_DKEY:(hc + 1) * PEER_DKEY].astype(BF16)


def _post(x2, cy, o2, om, gates, wco, wao, wmo, wo, ffn_norm, wq):
    n_tok = x2.shape[0]
    tm = min(TOKEN_TILE, n_tok)

    def rows(width):
        return pl.BlockSpec((tm, width), lambda i: (i, 0))

    return pl.pallas_call(
        _post_body,
        grid=(n_tok // tm,),
        in_specs=[rows(D_MODEL), rows(D_CONV), rows(D_ATTN), rows(D_MEM), rows(3 * D_MODEL),
                  _resident(wco.shape), _resident(wao.shape), _resident(wmo.shape),
                  _resident(wo.shape), _resident((1, D_MODEL)), _resident(wq.shape)],
        out_specs=(rows(D_MODEL), rows(D_MODEL),
                   pl.BlockSpec((2 * PEER_HEADS, tm, PEER_DKEY), lambda i: (0, i, 0))),
        out_shape=(jax.ShapeDtypeStruct((n_tok, D_MODEL), F32),
                   jax.ShapeDtypeStruct((n_tok, D_MODEL), BF16),
                   jax.ShapeDtypeStruct((2 * PEER_HEADS, n_tok, PEER_DKEY), BF16)),
        compiler_params=_params(1),
        name="post",
    )(x2, cy, o2, om, gates, wco, wao, wmo, wo, ffn_norm, wq)


_CAND_PIECES = (
    ((0, 1), (0, 16), 16), ((1, 2), (0, 8), 8), ((2, 3), (0, 8), 8), ((3, 4), (0, 8), 8),
    ((8, 16), (0, 1), 8), ((0, 8), (0, 1), 8), ((0, 8), (1, 2), 8), ((0, 8), (2, 3), 8))


def _route_body(qh_ref, sub_ref, i_ref, j_ref, g_ref,
                st1_ref, ix1_ref, st2_ref, ix2_ref, best_ref, si_ref, sj_ref, sg_ref):
    tm = qh_ref.shape[1]
    k = PEER_TOPK
    big = 1e9
    key_iota = lax.broadcasted_iota(jnp.int32, (PEER_NKEYS, tm), 0).astype(F32)

    def pop_max(x, r, st_ref, ix_ref):
        m = jnp.max(x, axis=0, keepdims=True)
        f = jnp.min(jnp.where(x == m, key_iota, big), axis=0, keepdims=True)
        st_ref[pl.ds(r, 1), :] = m
        ix_ref[pl.ds(r, 1), :] = f
        return jnp.where(key_iota == f, NEG_INF, x)

    def top_keys(h):
        x1 = _nt_dot(sub_ref[2 * h], qh_ref[2 * h])
        x2 = _nt_dot(sub_ref[2 * h + 1], qh_ref[2 * h + 1])

        def rnd(r, xs):
            return pop_max(xs[0], r, st1_ref, ix1_ref), pop_max(xs[1], r, st2_ref, ix2_ref)

        lax.fori_loop(0, k, rnd, (x1, x2))

    r8 = lax.broadcasted_iota(jnp.int32, (SUBLANES, tm), 0).astype(F32)
    r16 = lax.broadcasted_iota(jnp.int32, (2 * SUBLANES, tm), 0).astype(F32)
    fidx = jnp.concatenate([
        r16, k + r8, 2 * k + r8, 3 * k + r8,
        (r8 + SUBLANES) * k, r8 * k, r8 * k + 1, r8 * k + 2], axis=0)
    valid = jnp.concatenate([
        r16 >= 0, r8 >= 0, r8 < 5, r8 < 4,
        r8 >= 0, r8 >= 4, r8 >= 4, r8 == 4], axis=0)
    fidx = jnp.where(valid, fidx, -1.0)

    def head(h, carry):
        top_keys(h)

        def first(ref):
            return jnp.concatenate(
                [jnp.broadcast_to(ref[a0:a1, :], (n, tm)) for (a0, a1), _, n in _CAND_PIECES], axis=0)

        def second(ref):
            return jnp.concatenate(
                [jnp.broadcast_to(ref[b0:b1, :], (n, tm)) for _, (b0, b1), n in _CAND_PIECES], axis=0)

        cand = jnp.where(valid, first(st1_ref) + second(st2_ref), NEG_INF)
        ci = first(ix1_ref)
        cj = second(ix2_ref)

        def rnd(r, x):
            m = jnp.max(x, axis=0, keepdims=True)
            f = jnp.min(jnp.where(x == m, fidx, big), axis=0, keepdims=True)
            hit = fidx == f
            best_ref[pl.ds(r, 1), :] = m
            row = pl.multiple_of(h * k, k) + r
            si_ref[pl.ds(row, 1), :] = jnp.sum(jnp.where(hit, ci, 0.0), axis=0, keepdims=True)
            sj_ref[pl.ds(row, 1), :] = jnp.sum(jnp.where(hit, cj, 0.0), axis=0, keepdims=True)
            return jnp.where(hit, NEG_INF, x)

        lax.fori_loop(0, k, rnd, cand)
        best = best_ref[...]
        e = jnp.exp(best - best[0:1, :])
        sg_ref[pl.ds(pl.multiple_of(h * k, k), k), :] = e / jnp.sum(e, axis=0, keepdims=True)
        return carry

    lax.fori_loop(0, PEER_HEADS, head, 0)
    i_ref[...] = si_ref[...].T
    j_ref[...] = sj_ref[...].T
    g_ref[...] = sg_ref[...].T


def _route(qh3, subkeys):
    n_tok = qh3.shape[1]
    tm = min(ROUTE_TILE, n_tok)
    n_slots = PEER_HEADS * PEER_TOPK
    assert n_slots == LANES
    sub3 = subkeys.reshape(2 * PEER_HEADS, PEER_NKEYS, PEER_DKEY).astype(BF16)
    slot_spec = pl.BlockSpec((tm, n_slots), lambda i: (i, 0))
    small = pltpu.VMEM((PEER_TOPK, tm), F32)
    slots = pltpu.VMEM((n_slots, tm), F32)
    return pl.pallas_call(
        _route_body,
        grid=(n_tok // tm,),
        in_specs=[pl.BlockSpec((2 * PEER_HEADS, tm, PEER_DKEY), lambda i: (0, i, 0)),
                  _resident(sub3.shape)],
        out_specs=(slot_spec,) * 3,
        out_shape=(jax.ShapeDtypeStruct((n_tok, n_slots), F32),) * 3,
        scratch_shapes=[small, small, small, small, small, slots, slots, slots],
        compiler_params=_params(1),
        name="peer_route",
    )(qh3, sub3)


WBUILD_GROUP = 16
WBUILD_PITCH = PEER_NKEYS + SUBLANES


def _wbuild_body(i_ref, j_ref, g_ref, w_ref, stage_ref):
    tm = i_ref.shape[0]
    key = lax.broadcasted_iota(jnp.int32, (PEER_NKEYS, LANES), 0).astype(F32)

    def group(gi, carry):
        base = pl.multiple_of(gi * WBUILD_GROUP, WBUILD_GROUP)
        for tt in range(WBUILD_GROUP):
            irow = i_ref[pl.ds(base + tt, 1), :]
            jrow = j_ref[pl.ds(base + tt, 1), :]
            grow = g_ref[pl.ds(base + tt, 1), :]
            a = jnp.where(key == irow, grow, 0.0).astype(BF16)
            b = jnp.where(key == jrow, 1.0, 0.0).astype(BF16)
            stage_ref[tt * WBUILD_PITCH:tt * WBUILD_PITCH + PEER_NKEYS, :] = _nt_dot(a, b)
        for i in range(PEER_NKEYS):
            tile = stage_ref[pl.ds(i, WBUILD_GROUP, stride=WBUILD_PITCH), :]
            w_ref[pl.ds(base, WBUILD_GROUP), i * PEER_NKEYS:(i + 1) * PEER_NKEYS] = tile.astype(BF16)
        return carry

    lax.fori_loop(0, tm // WBUILD_GROUP, group, 0)


def _wbuild(si, sj, sg):
    n_tok = si.shape[0]
    tm = min(WBUILD_TILE, n_tok)
    assert tm % WBUILD_GROUP == 0
    slot_spec = pl.BlockSpec((tm, LANES), lambda i: (i, 0))
    return pl.pallas_call(
        _wbuild_body,
        grid=(n_tok // tm,),
        in_specs=[slot_spec] * 3,
        out_specs=pl.BlockSpec((tm, PEER_N_EXPERTS), lambda i: (i, 0)),
        out_shape=jax.ShapeDtypeStruct((n_tok, PEER_N_EXPERTS), BF16),
        scratch_shapes=[pltpu.VMEM((WBUILD_GROUP * WBUILD_PITCH, PEER_NKEYS), F32)],
        compiler_params=_params(1),
        name="peer_wbuild",
    )(si, sj, sg)


def _peer_body(xn_ref, u_ref, v_ref, w_ref, h_ref, y_ref, acc_ref):
    c = pl.program_id(1)

    @pl.when(c == 0)
    def _():
        acc_ref[...] = jnp.zeros(acc_ref.shape, F32)

    xn = xn_ref[...]
    sub = PEER_EXPERT_SUB
    for s in range(u_ref.shape[0] // sub):
        sl = slice(s * sub, (s + 1) * sub)
        z = _nt_dot(xn, u_ref[sl, :])
        a = 0.5 * z * (1.0 + lax.erf(z * (2.0 ** -0.5)))
        a = a * w_ref[:, sl].astype(F32)
        acc_ref[...] += jnp.dot(a.astype(BF16), v_ref[sl, :], preferred_element_type=F32)

    @pl.when(c == pl.num_programs(1) - 1)
    def _():
        y_ref[...] = h_ref[...] + acc_ref[...]


def _peer_dense(xn, w2, h, peer_u, peer_v):
    n_tok = xn.shape[0]
    tm = min(PEER_TOKEN_TILE, n_tok)
    ec = PEER_EXPERT_CHUNK
    assert n_tok % tm == 0 and PEER_N_EXPERTS % ec == 0 and ec % PEER_EXPERT_SUB == 0
    tok_spec = pl.BlockSpec((tm, D_MODEL), lambda i, c: (i, 0))
    exp_spec = pl.BlockSpec((ec, D_MODEL), lambda i, c: (c, 0))
    return pl.pallas_call(
        _peer_body,
        grid=(n_tok // tm, PEER_N_EXPERTS // ec),
        in_specs=[tok_spec, exp_spec, exp_spec,
                  pl.BlockSpec((tm, ec), lambda i, c: (i, c)), tok_spec],
        out_specs=tok_spec,
        out_shape=jax.ShapeDtypeStruct((n_tok, D_MODEL), F32),
        scratch_shapes=[pltpu.VMEM((tm, D_MODEL), F32)],
        compiler_params=_params(2),
        name="peer_dense",
    )(xn, peer_u, peer_v, w2, h)


def _row(v):
    return v.reshape(1, -1)


def _channel_mixer(x2, cy, o2, om, gates, lw):
    h, xn, qh3 = _post(x2, cy, o2, om, gates, lw["wco"], lw["wao"], lw["wmo"], lw["wo"],
                       lw["ffn_norm"], lw["wq"])
    si, sj, sg = _route(qh3, lw["subkeys"])
    w2 = _wbuild(si, sj, sg)
    return _peer_dense(xn, w2, h, lw["peer_u"], lw["peer_v"])


def kernel(x_prompt, x_sample, mem_prompt, cache_k, cache_v, cache_conv, cache_mem_k, cache_mem_v,
           page_table, attn_norm, w_in, conv_w, w_conv_out, q_norm, k_norm, w_attn_out,
           mem_norm, w_mem_kv, mk_norm, mq_norm, w_mem_out, w_o,
           ffn_norm, peer_wq, peer_subkeys, peer_u, peer_v):
    n_p, t_p, _ = x_prompt.shape
    n_s, t_s, _ = x_sample.shape
    depth = attn_norm.shape[0]
    past_len = page_table.shape[1] * cache_k.shape[2]
    h_p = x_prompt.reshape(n_p * t_p, D_MODEL)
    h_s = x_sample.reshape(n_s * t_s, D_MODEL)
    outs = [[] for _ in range(8)]
    for l in range(depth):
        lw = dict(
            wco=w_conv_out[l].astype(BF16), wao=w_attn_out[l].astype(BF16),
            wmo=w_mem_out[l].astype(BF16), wo=w_o[l].astype(BF16), wq=peer_wq[l].astype(BF16),
            ffn_norm=_row(ffn_norm[l]), subkeys=peer_subkeys[l],
            peer_u=peer_u[l].astype(BF16), peer_v=peer_v[l].astype(BF16))
        in_w = (_row(attn_norm[l]), w_in[l].astype(BF16), _row(q_norm[l]), _row(k_norm[l]),
                _row(mq_norm[l]), conv_w[l])

        mem_k, mem_v = _memkv(mem_prompt.reshape(n_p * N_MEM, D_MODEL), _row(mem_norm[l]),
                              w_mem_kv[l].astype(BF16), _row(mk_norm[l]))
        zero_state = jnp.zeros((SUBLANES, D_CONV), F32)
        cy, u, q, k, v, mq, gates, kmean = _inproj(h_p, *in_w, zero_state, zero_state, seq_len=t_p)
        o = _moba_prompt(q, k, v, kmean, n_seq=n_p, seq_len=t_p)
        om = _memattn(mq, mem_k.reshape(n_p, N_MEM, D_MEM), mem_v.reshape(n_p, N_MEM, D_MEM),
                      groups=1, rows=TOKEN_TILE, steps_per_mem=t_p // TOKEN_TILE)
        h_p = _channel_mixer(h_p, cy, o, om, gates, lw)
        outs[0].append(k.reshape(n_p, t_p, N_HEADS, HEAD_DIM))
        outs[1].append(v.reshape(n_p, t_p, N_HEADS, HEAD_DIM))
        outs[2].append(u.reshape(n_p, t_p, D_CONV)[:, t_p - (CONV_W - 1):])
        outs[3].append(mem_k.reshape(n_p, N_MEM, MEM_HEADS, MEM_HEAD_DIM))
        outs[4].append(mem_v.reshape(n_p, N_MEM, MEM_HEADS, MEM_HEAD_DIM))

        state = cache_conv[l]
        s0x = jnp.repeat(state[:, 0], t_s, axis=0)
        s1x = jnp.repeat(state[:, 1], t_s, axis=0)
        cy, u, q, k, v, mq, gates, _ = _inproj(h_s, *in_w, s0x, s1x, seq_len=t_s)
        k4 = k.reshape(n_s, t_s, N_HEADS, HEAD_DIM)
        v4 = v.reshape(n_s, t_s, N_HEADS, HEAD_DIM)
        rows = n_s * t_s * N_HEADS
        o = _moba_sample(q.reshape(rows, HEAD_DIM), k4.reshape(rows, HEAD_DIM),
                         v4.reshape(rows, HEAD_DIM), cache_k, cache_v, page_table, l,
                         n_seq=n_s, n_new=t_s, past_len=past_len)
        o = o.reshape(n_s * t_s, D_ATTN)
        groups = SUBLANES
        om = _memattn(mq, cache_mem_k[l].reshape(n_s, N_MEM, D_MEM),
                      cache_mem_v[l].reshape(n_s, N_MEM, D_MEM),
                      groups=groups, rows=t_s, steps_per_mem=1)
        h_s = _channel_mixer(h_s, cy, o, om, gates, lw)
        outs[5].append(k4)
        outs[6].append(v4)
        ext = jnp.concatenate([state, u.reshape(n_s, t_s, D_CONV)], axis=1)
        outs[7].append(ext[:, t_s:])
    stacked = [jnp.stack(o) for o in outs]
    return (h_p.reshape(n_p, t_p, D_MODEL), h_s.reshape(n_s, t_s, D_MODEL), *stacked)
```

```python
import functools

import jax
import jax.numpy as jnp
from jax import lax
from jax.experimental import pallas as pl
from jax.experimental.pallas import tpu as pltpu

F32 = jnp.float32
BF16 = jnp.bfloat16
HIGHEST = lax.Precision.HIGHEST
NEG_INF = float("-inf")
LOG2_E = 1.4426950408889634

SUBLANES = 8
LANES = 128
VMEM_LIMIT_BYTES = 56 * 1024 * 1024

EPS = 1e-6
D_MODEL = 1024
D_CONV = 512
CONV_W = 3
N_HEADS = 8
HEAD_DIM = 128
D_ATTN = N_HEADS * HEAD_DIM
MOBA_BLOCK = 256
MOBA_TOPK = 3
N_MEM = 256
MEM_HEADS = 4
MEM_HEAD_DIM = 128
D_MEM = MEM_HEADS * MEM_HEAD_DIM
PEER_HEADS = 8
PEER_NKEYS = 128
PEER_DKEY = 128
PEER_TOPK = 16
PEER_N_EXPERTS = PEER_NKEYS * PEER_NKEYS
D_PEER_Q = PEER_HEADS * 2 * PEER_DKEY
D_IN = 3 * D_CONV + 3 * D_ATTN + D_MEM + 3 * D_MODEL
OFF_CB, OFF_CC, OFF_CH = 0, D_CONV, 2 * D_CONV
OFF_Q = 3 * D_CONV
OFF_K = OFF_Q + D_ATTN
OFF_V = OFF_K + D_ATTN
OFF_MQ = OFF_V + D_ATTN
OFF_G = OFF_MQ + D_MEM

TOKEN_TILE = 256
ROUTE_TILE = 1024
WBUILD_TILE = 256
PEER_TOKEN_TILE = 512
PEER_EXPERT_CHUNK = 2048
PEER_EXPERT_SUB = 512


def _nt_dot(a, b, precision=None):
    return lax.dot_general(a, b, (((1,), (1,)), ((), ())), precision=precision,
                           preferred_element_type=F32)


def _params(n_grid_dims):
    return pltpu.CompilerParams(dimension_semantics=("arbitrary",) * n_grid_dims,
                                vmem_limit_bytes=VMEM_LIMIT_BYTES)


def _rms(x, gain):
    r = lax.rsqrt(jnp.mean(x * x, axis=-1, keepdims=True) + EPS)
    return x * r * gain


def _resident(shape):
    nd = len(shape)
    return pl.BlockSpec(shape, lambda *_: (0,) * nd, pipeline_mode=pl.Buffered(1))


def _inproj_body(x_ref, an_ref, w_ref, qn_ref, kn_ref, mqn_ref, cw_ref, s0_ref, s1_ref,
                 cy_ref, u_ref, q_ref, k_ref, v_ref, mq_ref, gate_ref, kmean_ref,
                 ubuf_ref, *, tiles_per_seq, seq_len):
    tm = x_ref.shape[0]
    xn = _rms(x_ref[...], an_ref[...]).astype(BF16)

    def proj(off, width):
        return jnp.dot(xn, w_ref[:, off:off + width], preferred_element_type=F32)

    cb = proj(OFF_CB, D_CONV)
    u = proj(OFF_CC, D_CONV) * proj(OFF_CH, D_CONV)
    u_ref[...] = u
    if seq_len >= tm:
        @pl.when(pl.program_id(0) % tiles_per_seq == 0)
        def _():
            ubuf_ref[0:SUBLANES, :] = jnp.zeros((SUBLANES, D_CONV), F32)
    else:
        ubuf_ref[0:SUBLANES, :] = jnp.zeros((SUBLANES, D_CONV), F32)
    ubuf_ref[SUBLANES:SUBLANES + tm, :] = u
    um1 = ubuf_ref[SUBLANES - 1:SUBLANES - 1 + tm, :]
    um2 = ubuf_ref[SUBLANES - 2:SUBLANES - 2 + tm, :]
    if seq_len >= tm:
        ubuf_ref[0:SUBLANES, :] = ubuf_ref[tm:tm + SUBLANES, :]
    else:
        assert seq_len == SUBLANES
        tl = lax.broadcasted_iota(jnp.int32, (tm, D_CONV), 0) % seq_len
        s0 = s0_ref[...]
        s1 = s1_ref[...]
        um1 = jnp.where(tl == 0, s1, um1)
        um2 = jnp.where(tl == 0, s0, jnp.where(tl == 1, s1, um2))
    cw = cw_ref[...]
    y = cw[0:1, :] * um2 + cw[1:2, :] * um1 + cw[2:3, :] * u
    cy_ref[...] = (cb * y).astype(BF16)

    qf = proj(OFF_Q, D_ATTN)
    kf = proj(OFF_K, D_ATTN)
    qn = qn_ref[...]
    kn = kn_ref[...]
    for h in range(N_HEADS):
        sl = slice(h * HEAD_DIM, (h + 1) * HEAD_DIM)
        q_ref[:, sl] = _rms(qf[:, sl], qn)
        k_ref[:, sl] = _rms(kf[:, sl], kn)
    kmean_ref[0] = jnp.mean(k_ref[...], axis=0, keepdims=True)
    v_ref[...] = proj(OFF_V, D_ATTN)
    mqf = proj(OFF_MQ, D_MEM)
    mqn = mqn_ref[...]
    for h in range(MEM_HEADS):
        sl = slice(h * MEM_HEAD_DIM, (h + 1) * MEM_HEAD_DIM)
        mq_ref[:, sl] = _rms(mqf[:, sl], mqn)
    gate_ref[...] = proj(OFF_G, 3 * D_MODEL)


def _inproj(x2, attn_norm, w_in, q_norm, k_norm, mq_norm, conv_w, s0x, s1x, *, seq_len):
    n_tok = x2.shape[0]
    tm = min(TOKEN_TILE, n_tok)
    assert n_tok % tm == 0 and (seq_len % tm == 0 or seq_len == SUBLANES)
    n_tiles = n_tok // tm
    tiles_per_seq = max(seq_len // tm, 1)
    long_seq = seq_len >= tm

    def rows(width):
        return pl.BlockSpec((tm, width), lambda i: (i, 0))

    state_spec = _resident(s0x.shape) if long_seq else rows(D_CONV)
    out_shape = (
        jax.ShapeDtypeStruct((n_tok, D_CONV), BF16),
        jax.ShapeDtypeStruct((n_tok, D_CONV), F32),
        jax.ShapeDtypeStruct((n_tok, D_ATTN), F32),
        jax.ShapeDtypeStruct((n_tok, D_ATTN), F32),
        jax.ShapeDtypeStruct((n_tok, D_ATTN), F32),
        jax.ShapeDtypeStruct((n_tok, D_MEM), F32),
        jax.ShapeDtypeStruct((n_tok, 3 * D_MODEL), F32),
        jax.ShapeDtypeStruct((n_tiles, 1, D_ATTN), F32),
    )
    return pl.pallas_call(
        functools.partial(_inproj_body, tiles_per_seq=tiles_per_seq, seq_len=seq_len),
        grid=(n_tiles,),
        in_specs=[rows(D_MODEL), _resident((1, D_MODEL)), _resident((D_MODEL, D_IN)),
                  _resident((1, HEAD_DIM)), _resident((1, HEAD_DIM)), _resident((1, MEM_HEAD_DIM)),
                  _resident((CONV_W, D_CONV)), state_spec, state_spec],
        out_specs=(rows(D_CONV), rows(D_CONV), rows(D_ATTN), rows(D_ATTN), rows(D_ATTN),
                   rows(D_MEM), rows(3 * D_MODEL),
                   pl.BlockSpec((1, 1, D_ATTN), lambda i: (i, 0, 0))),
        out_shape=out_shape,
        scratch_shapes=[pltpu.VMEM((tm + 2 * SUBLANES, D_CONV), F32)],
        compiler_params=_params(1),
        name="inproj",
    )(x2, attn_norm, w_in, q_norm, k_norm, mq_norm, conv_w, s0x, s1x)


def _moba_prompt_body(q_ref, k_ref, v_ref, km_ref, o_ref, kb_ref, vt_ref, sc_ref, m_ref,
                      *, n_blocks):
    s_id = pl.program_id(2)
    blk = MOBA_BLOCK
    scale = HEAD_DIM ** -0.5

    @pl.when(s_id == 0)
    def _():
        kb_ref[...] = k_ref[...].astype(BF16)
        for j in range(n_blocks):
            vt_ref[:, j * blk:(j + 1) * blk] = v_ref[j * blk:(j + 1) * blk, :].T.astype(BF16)

    key_row = lax.broadcasted_iota(jnp.int32, (blk, blk), 0)
    query_col = lax.broadcasted_iota(jnp.int32, (blk, blk), 1)

    def score_block(own):
        q = q_ref[...]
        gate = _nt_dot(km_ref[0], q, precision=HIGHEST)
        blk_id = lax.broadcasted_iota(jnp.int32, gate.shape, 0)
        rank = jnp.zeros(gate.shape, F32)
        for jp in range(own):
            other = gate[jp:jp + 1, :]
            beats = (other > gate) | ((other == gate) & (jp < blk_id))
            rank = rank + jnp.where(beats, 1.0, 0.0)
        k_sel = min(MOBA_TOPK, n_blocks)
        sel = jnp.where((rank < k_sel) & (blk_id < own), 1.0, 0.0)
        qt = (q * (scale * LOG2_E)).T.astype(BF16)
        buf = own % 2
        m = None
        for j in range(own + 1):
            s = jnp.dot(kb_ref[j * blk:(j + 1) * blk, :], qt, preferred_element_type=F32)
            if j == own:
                s = jnp.where(key_row <= query_col, s, NEG_INF)
            else:
                s = jnp.where(sel[j:j + 1, :] > 0.5, s, NEG_INF)
            sc_ref[buf, j] = s
            mj = jnp.max(s, axis=0, keepdims=True)
            m = mj if m is None else jnp.maximum(m, mj)
        m_ref[buf] = m

    def finish_block(own):
        buf = own % 2
        m = m_ref[buf]
        l = jnp.zeros((1, blk), F32)
        acc = jnp.zeros((HEAD_DIM, blk), F32)
        for j in range(own + 1):
            p = jnp.exp2(sc_ref[buf, j] - m)
            l = l + jnp.sum(p, axis=0, keepdims=True)
            acc = acc + jnp.dot(vt_ref[:, j * blk:(j + 1) * blk], p.astype(BF16),
                                preferred_element_type=F32)
        o_ref[...] = (acc / l).T.astype(o_ref.dtype)

    for step in range(n_blocks + 1):
        @pl.when(s_id == step)
        def _(step=step):
            if step < n_blocks:
                score_block(step)
            if step >= 1:
                finish_block(step - 1)


def _moba_prompt(q2, k2, v2, kmean, *, n_seq, seq_len):
    blk = MOBA_BLOCK
    assert seq_len % blk == 0 and TOKEN_TILE == blk
    nb = seq_len // blk
    km = kmean.reshape(n_seq, nb, D_ATTN)
    kv_spec = pl.BlockSpec((seq_len, HEAD_DIM), lambda n, h, s: (n, h))
    return pl.pallas_call(
        functools.partial(_moba_prompt_body, n_blocks=nb),
        grid=(n_seq, N_HEADS, nb + 1),
        in_specs=[pl.BlockSpec((blk, HEAD_DIM), lambda n, h, s: (n * nb + jnp.minimum(s, nb - 1), h)),
                  kv_spec, kv_spec,
                  pl.BlockSpec((1, nb, HEAD_DIM), lambda n, h, s: (n, 0, h))],
        out_specs=pl.BlockSpec((blk, HEAD_DIM), lambda n, h, s: (n * nb + jnp.maximum(s - 1, 0), h)),
        out_shape=jax.ShapeDtypeStruct(q2.shape, BF16),
        scratch_shapes=[pltpu.VMEM((seq_len, HEAD_DIM), BF16), pltpu.VMEM((HEAD_DIM, seq_len), BF16),
                        pltpu.VMEM((2, nb, blk, blk), F32), pltpu.VMEM((2, 1, blk), F32)],
        compiler_params=_params(3),
        name="moba_prompt",
    )(q2, k2, v2, km)


def _moba_sample_body(pt_ref, q_ref, kn_ref, vn_ref, *rest, n_pages, page_size, n_new):
    del pt_ref
    kp = rest[:n_pages]
    vp = rest[n_pages:2 * n_pages]
    o_ref, s_ref = rest[2 * n_pages:]
    rows = n_new * N_HEADS
    page_rows = page_size * N_HEADS
    pages_per_block = MOBA_BLOCK // page_size
    n_past_blocks = n_pages // pages_per_block
    scale = HEAD_DIM ** -0.5

    q = q_ref[...]
    kmeans = []
    for blk in range(n_past_blocks):
        acc = jnp.zeros((N_HEADS, HEAD_DIM), F32)
        for p in range(blk * pages_per_block, (blk + 1) * pages_per_block):
            acc = acc + jnp.sum(kp[p][...], axis=0)
        kmeans.append(acc * (1.0 / MOBA_BLOCK))
    kmean = jnp.concatenate(kmeans, axis=0)
    gate = _nt_dot(q, kmean, precision=HIGHEST)
    r_head = lax.broadcasted_iota(jnp.int32, gate.shape, 0) % N_HEADS
    c_idx = lax.broadcasted_iota(jnp.int32, gate.shape, 1)
    c_blk = c_idx // N_HEADS
    same_head = (c_idx % N_HEADS) == r_head
    rank = jnp.zeros(gate.shape, F32)
    for bp in range(n_past_blocks):
        colval = jnp.max(jnp.where(same_head & (c_blk == bp), gate, NEG_INF), axis=1, keepdims=True)
        beats = (colval > gate) | ((colval == gate) & (bp < c_blk))
        rank = rank + jnp.where(beats, 1.0, 0.0)
    k_sel = min(MOBA_TOPK, n_past_blocks + 1)
    sel = jnp.where(same_head & (rank < k_sel), 1.0, 0.0)

    qb = q.astype(BF16)
    p_head = lax.broadcasted_iota(jnp.int32, (rows, page_rows), 0) % N_HEADS
    p_lane_head = lax.broadcasted_iota(jnp.int32, (rows, page_rows), 1) % N_HEADS
    page_same_head = p_head == p_lane_head
    for blk in range(n_past_blocks):
        blk_sel = jnp.max(jnp.where(c_blk == blk, sel, 0.0), axis=1, keepdims=True) > 0.5
        keep = page_same_head & blk_sel
        for p in range(blk * pages_per_block, (blk + 1) * pages_per_block):
            kpage = kp[p][...].reshape(page_rows, HEAD_DIM).astype(BF16)
            s = _nt_dot(qb, kpage) * scale
            s_ref[:, p * page_rows:(p + 1) * page_rows] = jnp.where(keep, s, NEG_INF)
    pad = jnp.zeros((LANES - rows, HEAD_DIM), F32)
    kn = jnp.concatenate([kn_ref[...], pad], axis=0).astype(BF16)
    vn = jnp.concatenate([vn_ref[...], pad], axis=0).astype(BF16)
    sn = _nt_dot(qb, kn) * scale
    n_row = lax.broadcasted_iota(jnp.int32, sn.shape, 0)
    n_col = lax.broadcasted_iota(jnp.int32, sn.shape, 1)
    keep_new = ((n_col % N_HEADS) == (n_row % N_HEADS)) & (n_col // N_HEADS <= n_row // N_HEADS) \
        & (n_col < rows)
    past = n_pages * page_rows
    s_ref[:, past:past + LANES] = jnp.where(keep_new, sn, NEG_INF)

    m = jnp.max(s_ref[...], axis=1, keepdims=True)
    l = jnp.zeros((rows, 1), F32)
    acc = jnp.zeros((rows, HEAD_DIM), F32)
    for p in range(n_pages):
        pp = jnp.exp(s_ref[:, p * page_rows:(p + 1) * page_rows] - m)
        l = l + jnp.sum(pp, axis=1, keepdims=True)
        vpage = vp[p][...].reshape(page_rows, HEAD_DIM).astype(BF16)
        acc = acc + jnp.dot(pp.astype(BF16), vpage, preferred_element_type=F32)
    pn = jnp.exp(s_ref[:, past:past + LANES] - m)
    l = l + jnp.sum(pn, axis=1, keepdims=True)
    acc = acc + jnp.dot(pn.astype(BF16), vn, preferred_element_type=F32)
    o_ref[...] = (acc / l).astype(o_ref.dtype)


def _moba_sample(q8, kn8, vn8, cache_k, cache_v, page_table, layer, *, n_seq, n_new, past_len):
    page_size = cache_k.shape[2]
    n_pages = page_table.shape[1]
    assert past_len == n_pages * page_size and past_len % MOBA_BLOCK == 0
    assert MOBA_BLOCK % page_size == 0 and n_new <= MOBA_BLOCK and n_new * N_HEADS <= LANES
    assert cache_k.shape[3:] == (N_HEADS, HEAD_DIM) and N_HEADS == SUBLANES
    rows = n_new * N_HEADS
    row_spec = pl.BlockSpec((rows, HEAD_DIM), lambda s, pt: (s, 0))

    def page_spec(p):
        return pl.BlockSpec((None, None, page_size, N_HEADS, HEAD_DIM),
                            lambda s, pt: (layer, pt[s, p], 0, 0, 0))

    page_specs = [page_spec(p) for p in range(n_pages)]
    grid_spec = pltpu.PrefetchScalarGridSpec(
        num_scalar_prefetch=1,
        grid=(n_seq,),
        in_specs=[row_spec, row_spec, row_spec] + page_specs + page_specs,
        out_specs=row_spec,
        scratch_shapes=[pltpu.VMEM((rows, n_pages * page_size * N_HEADS + LANES), F32)],
    )
    return pl.pallas_call(
        functools.partial(_moba_sample_body, n_pages=n_pages, page_size=page_size, n_new=n_new),
        grid_spec=grid_spec,
        out_shape=jax.ShapeDtypeStruct(q8.shape, BF16),
        compiler_params=_params(1),
        name="moba_sample",
    )(page_table, q8, kn8, vn8, *([cache_k] * n_pages), *([cache_v] * n_pages))


def _memkv_body(mem_ref, mn_ref, w_ref, mkn_ref, k_ref, v_ref):
    xn = _rms(mem_ref[...], mn_ref[...]).astype(BF16)
    kv = jnp.dot(xn, w_ref[...], preferred_element_type=F32)
    mkn = mkn_ref[...]
    for h in range(MEM_HEADS):
        sl = slice(h * MEM_HEAD_DIM, (h + 1) * MEM_HEAD_DIM)
        k_ref[:, sl] = _rms(kv[:, sl], mkn)
    v_ref[...] = kv[:, D_MEM:]


def _memkv(mem2, mem_norm, w_mem_kv, mk_norm):
    n_rows = mem2.shape[0]
    tm = N_MEM
    spec = pl.BlockSpec((tm, D_MEM), lambda i: (i, 0))
    return pl.pallas_call(
        _memkv_body,
        grid=(n_rows // tm,),
        in_specs=[pl.BlockSpec((tm, D_MODEL), lambda i: (i, 0)), _resident((1, D_MODEL)),
                  _resident((D_MODEL, 2 * D_MEM)), _resident((1, MEM_HEAD_DIM))],
        out_specs=(spec, spec),
        out_shape=(jax.ShapeDtypeStruct((n_rows, D_MEM), F32),) * 2,
        compiler_params=_params(1),
        name="memkv",
    )(mem2, mem_norm, w_mem_kv, mk_norm)


def _memattn_body(mq_ref, k_ref, v_ref, o_ref, *, groups, rows, heads_on_lanes):
    scale = MEM_HEAD_DIM ** -0.5
    for g in range(groups):
        rs = slice(g * rows, (g + 1) * rows)
        for h in range(MEM_HEADS):
            sl = slice(h * MEM_HEAD_DIM, (h + 1) * MEM_HEAD_DIM)
            if heads_on_lanes:
                kh, vh = k_ref[g, :, sl], v_ref[g, :, sl]
            else:
                kh, vh = k_ref[g, :, h, :], v_ref[g, :, h, :]
            qh = mq_ref[rs, sl].astype(BF16)
            s = _nt_dot(qh, kh.astype(BF16)) * scale
            p = jnp.exp(s - jnp.max(s, axis=1, keepdims=True))
            l = jnp.sum(p, axis=1, keepdims=True)
            o = jnp.dot(p.astype(BF16), vh.astype(BF16), preferred_element_type=F32)
            o_ref[rs, sl] = (o / l).astype(o_ref.dtype)


def _memattn(mq2, mem_k, mem_v, mem_spec, *, groups, rows):
    n_tok = mq2.shape[0]
    tm = groups * rows
    row_spec = pl.BlockSpec((tm, D_MEM), lambda i: (i, 0))
    heads_on_lanes = sum(d is not None for d in mem_spec.block_shape) == 3
    return pl.pallas_call(
        functools.partial(_memattn_body, groups=groups, rows=rows, heads_on_lanes=heads_on_lanes),
        grid=(n_tok // tm,),
        in_specs=[row_spec, mem_spec, mem_spec],
        out_specs=row_spec,
        out_shape=jax.ShapeDtypeStruct((n_tok, D_MEM), BF16),
        compiler_params=_params(1),
        name="memattn",
    )(mq2, mem_k, mem_v)


def _post_body(x_ref, cy_ref, o_ref, om_ref, gate_ref, wco_ref, wao_ref, wmo_ref, wo_ref,
               fn_ref, wq_ref, h_ref, xn_ref, qh_ref):
    conv_out = jnp.dot(cy_ref[...], wco_ref[...], preferred_element_type=F32)
    attn_out = jnp.dot(o_ref[...], wao_ref[...], preferred_element_type=F32)
    mem_out = jnp.dot(om_ref[...], wmo_ref[...], preferred_element_type=F32)
    merged = (jax.nn.sigmoid(gate_ref[:, 0:D_MODEL]) * conv_out
              + jax.nn.sigmoid(gate_ref[:, D_MODEL:2 * D_MODEL]) * attn_out
              + jax.nn.sigmoid(gate_ref[:, 2 * D_MODEL:3 * D_MODEL]) * mem_out)
    h = x_ref[...] + jnp.dot(merged.astype(BF16), wo_ref[...], preferred_element_type=F32)
    h_ref[...] = h
    xn = _rms(h, fn_ref[...]).astype(BF16)
    xn_ref[...] = xn
    qh = jnp.dot(xn, wq_ref[...], preferred_element_type=F32)
    for hc in range(2 * PEER_HEADS):
        qh_ref[hc] = qh[:, hc * PEER_DKEY:(hc + 1) * PEER_DKEY].astype(BF16)


def _post(x2, cy, o2, om, gates, wco, wao, wmo, wo, ffn_norm, wq):
    n_tok = x2.shape[0]
    tm = min(TOKEN_TILE, n_tok)

    def rows(width):
        return pl.BlockSpec((tm, width), lambda i: (i, 0))

    return pl.pallas_call(
        _post_body,
        grid=(n_tok // tm,),
        in_specs=[rows(D_MODEL), rows(D_CONV), rows(D_ATTN), rows(D_MEM), rows(3 * D_MODEL),
                  _resident(wco.shape), _resident(wao.shape), _resident(wmo.shape),
                  _resident(wo.shape), _resident((1, D_MODEL)), _resident(wq.shape)],
        out_specs=(rows(D_MODEL), rows(D_MODEL),
                   pl.BlockSpec((2 * PEER_HEADS, tm, PEER_DKEY), lambda i: (0, i, 0))),
        out_shape=(jax.ShapeDtypeStruct((n_tok, D_MODEL), F32),
                   jax.ShapeDtypeStruct((n_tok, D_MODEL), BF16),
                   jax.ShapeDtypeStruct((2 * PEER_HEADS, n_tok, PEER_DKEY), BF16)),
        compiler_params=_params(1),
        name="post",
    )(x2, cy, o2, om, gates, wco, wao, wmo, wo, ffn_norm, wq)


_CAND_PAIRS = tuple((a, b) for a in range(PEER_TOPK) for b in range(PEER_TOPK)
                    if (a + 1) * (b + 1) <= PEER_TOPK)


def _tree_argmax(val, idx):
    chunk = 2 * SUBLANES
    if val.shape[0] > chunk:
        parts = [_tree_argmax(val[c:c + chunk], idx[c:c + chunk])
                 for c in range(0, val.shape[0], chunk)]
        val = jnp.stack([p[0] for p in parts])
        idx = jnp.stack([p[1] for p in parts])
    while val.shape[0] > 1:
        n = val.shape[0]
        pairs = n // 2
        v2 = val[:2 * pairs].reshape((pairs, 2) + val.shape[1:])
        i2 = idx[:2 * pairs].reshape((pairs, 2) + idx.shape[1:])
        left = v2[:, 0] >= v2[:, 1]
        new_val = jnp.maximum(v2[:, 0], v2[:, 1])
        new_idx = jnp.where(left, i2[:, 0], i2[:, 1])
        if n % 2:
            new_val = jnp.concatenate([new_val, val[2 * pairs:]], axis=0)
            new_idx = jnp.concatenate([new_idx, idx[2 * pairs:]], axis=0)
        val, idx = new_val, new_idx
    return val[0], idx[0]


def _route_body(qh_ref, sub_ref, i_ref, j_ref, g_ref, x1_ref, x2_ref,
                st1_ref, ix1_ref, st2_ref, ix2_ref, best_ref, flat_ref, si_ref, sj_ref, sg_ref):
    tm = qh_ref.shape[1]
    n_sub = tm // LANES
    k = PEER_TOPK
    tile = (n_sub, LANES)
    key_id = lax.broadcasted_iota(jnp.int32, (PEER_NKEYS,) + tile, 0).astype(F32)

    def scores_into(hc, x_ref):
        for s in range(n_sub):
            blk = _nt_dot(sub_ref[hc], qh_ref[hc, s * LANES:(s + 1) * LANES, :])
            x_ref[:, s, :] = blk

    def pop_max(x_ref, r, st_ref, ix_ref):
        x = x_ref[...]
        m, f = _tree_argmax(x, key_id)
        st_ref[r] = m
        ix_ref[r] = f
        x_ref[...] = jnp.where(key_id == f[None], NEG_INF, x)

    flat_id = jnp.stack([jnp.full(tile, float(a * k + b), F32) for a, b in _CAND_PAIRS])

    def lookup(rank, table):
        out = jnp.zeros(rank.shape, F32)
        for a in range(k):
            out = jnp.where(rank == float(a), table[a][None], out)
        return out

    def head(h, carry):
        scores_into(2 * h, x1_ref)
        scores_into(2 * h + 1, x2_ref)

        def rnd1(r, c):
            pop_max(x1_ref, r, st1_ref, ix1_ref)
            pop_max(x2_ref, r, st2_ref, ix2_ref)
            return c

        lax.fori_loop(0, k, rnd1, 0)
        st1 = st1_ref[...]
        st2 = st2_ref[...]
        cand = jnp.stack([st1[a] + st2[b] for a, b in _CAND_PAIRS])

        def rnd2(r, x):
            m, f = _tree_argmax(x, flat_id)
            best_ref[r] = m
            flat_ref[r] = f
            return jnp.where(flat_id == f[None], NEG_INF, x)

        lax.fori_loop(0, k, rnd2, cand)
        best = best_ref[...]
        flat = flat_ref[...]
        rank_a = jnp.floor(flat * (1.0 / k))
        rank_b = flat - rank_a * k
        e = jnp.exp(best - best[0:1])
        rows = pl.ds(pl.multiple_of(h * k, k), k)
        si_ref[rows] = lookup(rank_a, ix1_ref[...])
        sj_ref[rows] = lookup(rank_b, ix2_ref[...])
        sg_ref[rows] = e / jnp.sum(e, axis=0, keepdims=True)
        return carry

    lax.fori_loop(0, PEER_HEADS, head, 0)
    for s in range(n_sub):
        rows = slice(s * LANES, (s + 1) * LANES)
        i_ref[rows, :] = si_ref[:, s, :].T
        j_ref[rows, :] = sj_ref[:, s, :].T
        g_ref[rows, :] = sg_ref[:, s, :].T


def _route(qh3, subkeys):
    n_tok = qh3.shape[1]
    tm = min(ROUTE_TILE, n_tok)
    n_slots = PEER_HEADS * PEER_TOPK
    assert n_slots == LANES and tm % LANES == 0 and n_tok % tm == 0
    tile = (tm // LANES, LANES)
    sub3 = subkeys.reshape(2 * PEER_HEADS, PEER_NKEYS, PEER_DKEY).astype(BF16)
    slot_spec = pl.BlockSpec((tm, n_slots), lambda i: (i, 0))
    keys = pltpu.VMEM((PEER_NKEYS,) + tile, F32)
    small = pltpu.VMEM((PEER_TOPK,) + tile, F32)
    slots = pltpu.VMEM((n_slots,) + tile, F32)
    return pl.pallas_call(
        _route_body,
        grid=(n_tok // tm,),
        in_specs=[pl.BlockSpec((2 * PEER_HEADS, tm, PEER_DKEY), lambda i: (0, i, 0)),
                  _resident(sub3.shape)],
        out_specs=(slot_spec,) * 3,
        out_shape=(jax.ShapeDtypeStruct((n_tok, n_slots), F32),) * 3,
        scratch_shapes=[keys, keys, small, small, small, small, small, small, slots, slots, slots],
        compiler_params=_params(1),
        name="peer_route",
    )(qh3, sub3)


WBUILD_GROUP = 16
WBUILD_PITCH = PEER_NKEYS + SUBLANES


def _wbuild_body(i_ref, j_ref, g_ref, w_ref, stage_a_ref, stage_b_ref):
    tm = i_ref.shape[0]
    key = lax.broadcasted_iota(jnp.int32, (PEER_NKEYS, LANES), 0).astype(F32)

    def build(gi, stage_ref):
        base = pl.multiple_of(gi * WBUILD_GROUP, WBUILD_GROUP)
        for tt in range(WBUILD_GROUP):
            irow = i_ref[pl.ds(base + tt, 1), :]
            jrow = j_ref[pl.ds(base + tt, 1), :]
            grow = g_ref[pl.ds(base + tt, 1), :]
            a = jnp.where(key == irow, grow, 0.0).astype(BF16)
            b = jnp.where(key == jrow, 1.0, 0.0).astype(BF16)
            stage_ref[tt * WBUILD_PITCH:tt * WBUILD_PITCH + PEER_NKEYS, :] = _nt_dot(a, b)

    def drain(gi, stage_ref):
        base = pl.multiple_of(gi * WBUILD_GROUP, WBUILD_GROUP)
        for i in range(PEER_NKEYS):
            tile = stage_ref[pl.ds(i, WBUILD_GROUP, stride=WBUILD_PITCH), :]
            w_ref[pl.ds(base, WBUILD_GROUP), i * PEER_NKEYS:(i + 1) * PEER_NKEYS] = tile.astype(BF16)

    n_groups = tm // WBUILD_GROUP
    assert n_groups % 2 == 0
    build(0, stage_a_ref)

    def step(p, carry):
        build(2 * p + 1, stage_b_ref)
        drain(2 * p, stage_a_ref)
        build(2 * p + 2, stage_a_ref)
        drain(2 * p + 1, stage_b_ref)
        return carry

    lax.fori_loop(0, n_groups // 2 - 1, step, 0)
    build(n_groups - 1, stage_b_ref)
    drain(n_groups - 2, stage_a_ref)
    drain(n_groups - 1, stage_b_ref)


def _wbuild(si, sj, sg):
    n_tok = si.shape[0]
    tm = min(WBUILD_TILE, n_tok)
    assert tm % WBUILD_GROUP == 0
    slot_spec = pl.BlockSpec((tm, LANES), lambda i: (i, 0))
    return pl.pallas_call(
        _wbuild_body,
        grid=(n_tok // tm,),
        in_specs=[slot_spec] * 3,
        out_specs=pl.BlockSpec((tm, PEER_N_EXPERTS), lambda i: (i, 0)),
        out_shape=jax.ShapeDtypeStruct((n_tok, PEER_N_EXPERTS), BF16),
        scratch_shapes=[pltpu.VMEM((WBUILD_GROUP * WBUILD_PITCH, PEER_NKEYS), F32)] * 2,
        compiler_params=_params(1),
        name="peer_wbuild",
    )(si, sj, sg)


def _peer_body(xn_ref, u_ref, v_ref, w_ref, h_ref, y_ref, acc_ref):
    c = pl.program_id(1)

    @pl.when(c == 0)
    def _():
        acc_ref[...] = jnp.zeros(acc_ref.shape, F32)

    xn = xn_ref[...]
    sub = PEER_EXPERT_SUB
    for s in range(u_ref.shape[0] // sub):
        sl = slice(s * sub, (s + 1) * sub)
        z = _nt_dot(xn, u_ref[sl, :])
        a = 0.5 * z * (1.0 + lax.erf(z * (2.0 ** -0.5)))
        a = a * w_ref[:, sl].astype(F32)
        acc_ref[...] += jnp.dot(a.astype(BF16), v_ref[sl, :], preferred_element_type=F32)

    @pl.when(c == pl.num_programs(1) - 1)
    def _():
        y_ref[...] = h_ref[...] + acc_ref[...]


def _peer_dense(xn, w2, h, peer_u, peer_v):
    n_tok = xn.shape[0]
    tm = min(PEER_TOKEN_TILE, n_tok)
    ec = PEER_EXPERT_CHUNK
    assert n_tok % tm == 0 and PEER_N_EXPERTS % ec == 0 and ec % PEER_EXPERT_SUB == 0
    tok_spec = pl.BlockSpec((tm, D_MODEL), lambda i, c: (i, 0))
    exp_spec = pl.BlockSpec((ec, D_MODEL), lambda i, c: (c, 0))
    return pl.pallas_call(
        _peer_body,
        grid=(n_tok // tm, PEER_N_EXPERTS // ec),
        in_specs=[tok_spec, exp_spec, exp_spec,
                  pl.BlockSpec((tm, ec), lambda i, c: (i, c)), tok_spec],
        out_specs=tok_spec,
        out_shape=jax.ShapeDtypeStruct((n_tok, D_MODEL), F32),
        scratch_shapes=[pltpu.VMEM((tm, D_MODEL), F32)],
        compiler_params=_params(2),
        name="peer_dense",
    )(xn, peer_u, peer_v, w2, h)


def _row(v):
    return v.reshape(1, -1)


def _channel_mixer(x2, cy, o2, om, gates, lw):
    h, xn, qh3 = _post(x2, cy, o2, om, gates, lw["wco"], lw["wao"], lw["wmo"], lw["wo"],
                       lw["ffn_norm"], lw["wq"])
    si, sj, sg = _route(qh3, lw["subkeys"])
    w2 = _wbuild(si, sj, sg)
    return _peer_dense(xn, w2, h, lw["peer_u"], lw["peer_v"])


def kernel(x_prompt, x_sample, mem_prompt, cache_k, cache_v, cache_conv, cache_mem_k, cache_mem_v,
           page_table, attn_norm, w_in, conv_w, w_conv_out, q_norm, k_norm, w_attn_out,
           mem_norm, w_mem_kv, mk_norm, mq_norm, w_mem_out, w_o,
           ffn_norm, peer_wq, peer_subkeys, peer_u, peer_v):
    n_p, t_p, _ = x_prompt.shape
    n_s, t_s, _ = x_sample.shape
    depth = attn_norm.shape[0]
    past_len = page_table.shape[1] * cache_k.shape[2]
    h_p = x_prompt.reshape(n_p * t_p, D_MODEL)
    h_s = x_sample.reshape(n_s * t_s, D_MODEL)
    outs = [[] for _ in range(8)]
    for l in range(depth):
        lw = dict(
            wco=w_conv_out[l].astype(BF16), wao=w_attn_out[l].astype(BF16),
            wmo=w_mem_out[l].astype(BF16), wo=w_o[l].astype(BF16), wq=peer_wq[l].astype(BF16),
            ffn_norm=_row(ffn_norm[l]), subkeys=peer_subkeys[l],
            peer_u=peer_u[l].astype(BF16), peer_v=peer_v[l].astype(BF16))
        in_w = (_row(attn_norm[l]), w_in[l].astype(BF16), _row(q_norm[l]), _row(k_norm[l]),
                _row(mq_norm[l]), conv_w[l])

        mem_k, mem_v = _memkv(mem_prompt.reshape(n_p * N_MEM, D_MODEL), _row(mem_norm[l]),
                              w_mem_kv[l].astype(BF16), _row(mk_norm[l]))
        zero_state = jnp.zeros((SUBLANES, D_CONV), F32)
        cy, u, q, k, v, mq, gates, kmean = _inproj(h_p, *in_w, zero_state, zero_state, seq_len=t_p)
        o = _moba_prompt(q, k, v, kmean, n_seq=n_p, seq_len=t_p)
        steps_per_mem = t_p // TOKEN_TILE

        prompt_mem = pl.BlockSpec((1, N_MEM, D_MEM), lambda i: (i // steps_per_mem, 0, 0))
        om = _memattn(mq, mem_k.reshape(n_p, N_MEM, D_MEM), mem_v.reshape(n_p, N_MEM, D_MEM),
                      prompt_mem, groups=1, rows=TOKEN_TILE)
        h_p = _channel_mixer(h_p, cy, o, om, gates, lw)
        outs[0].append(k.reshape(n_p, t_p, N_HEADS, HEAD_DIM))
        outs[1].append(v.reshape(n_p, t_p, N_HEADS, HEAD_DIM))
        outs[2].append(u.reshape(n_p, t_p, D_CONV)[:, t_p - (CONV_W - 1):])
        outs[3].append(mem_k.reshape(n_p, N_MEM, MEM_HEADS, MEM_HEAD_DIM))
        outs[4].append(mem_v.reshape(n_p, N_MEM, MEM_HEADS, MEM_HEAD_DIM))

        state = cache_conv[l]
        s0x = jnp.repeat(state[:, 0], t_s, axis=0)
        s1x = jnp.repeat(state[:, 1], t_s, axis=0)
        cy, u, q, k, v, mq, gates, _ = _inproj(h_s, *in_w, s0x, s1x, seq_len=t_s)
        k4 = k.reshape(n_s, t_s, N_HEADS, HEAD_DIM)
        v4 = v.reshape(n_s, t_s, N_HEADS, HEAD_DIM)
        rows = n_s * t_s * N_HEADS
        o = _moba_sample(q.reshape(rows, HEAD_DIM), k4.reshape(rows, HEAD_DIM),
                         v4.reshape(rows, HEAD_DIM), cache_k, cache_v, page_table, l,
                         n_seq=n_s, n_new=t_s, past_len=past_len)
        o = o.reshape(n_s * t_s, D_ATTN)
        groups = SUBLANES

        sample_mem = pl.BlockSpec((None, groups, N_MEM, MEM_HEADS, MEM_HEAD_DIM),
                                  lambda i: (l, i, 0, 0, 0))
        om = _memattn(mq, cache_mem_k, cache_mem_v, sample_mem, groups=groups, rows=t_s)
        h_s = _channel_mixer(h_s, cy, o, om, gates, lw)
        outs[5].append(k4)
        outs[6].append(v4)
        ext = jnp.concatenate([state, u.reshape(n_s, t_s, D_CONV)], axis=1)
        outs[7].append(ext[:, t_s:])
    stacked = [jnp.stack(o) for o in outs]
    return (h_p.reshape(n_p, t_p, D_MODEL), h_s.reshape(n_s, t_s, D_MODEL), *stacked)
```

```python
import functools

import jax
import jax.numpy as jnp
from jax import lax
from jax.experimental import pallas as pl
from jax.experimental.pallas import tpu as pltpu

F32 = jnp.float32
BF16 = jnp.bfloat16
HIGHEST = lax.Precision.HIGHEST
NEG_INF = float("-inf")
LOG2_E = 1.4426950408889634

SUBLANES = 8
LANES = 128
VMEM_LIMIT_BYTES = 56 * 1024 * 1024

EPS = 1e-6
D_MODEL = 1024
D_CONV = 512
CONV_W = 3
N_HEADS = 8
HEAD_DIM = 128
D_ATTN = N_HEADS * HEAD_DIM
MOBA_BLOCK = 256
MOBA_TOPK = 3
N_MEM = 256
MEM_HEADS = 4
MEM_HEAD_DIM = 128
D_MEM = MEM_HEADS * MEM_HEAD_DIM
PEER_HEADS = 8
PEER_NKEYS = 128
PEER_DKEY = 128
PEER_TOPK = 16
PEER_N_EXPERTS = PEER_NKEYS * PEER_NKEYS
D_PEER_Q = PEER_HEADS * 2 * PEER_DKEY
D_IN = 3 * D_CONV + 3 * D_ATTN + D_MEM + 3 * D_MODEL
OFF_CB, OFF_CC, OFF_CH = 0, D_CONV, 2 * D_CONV
OFF_Q = 3 * D_CONV
OFF_K = OFF_Q + D_ATTN
OFF_V = OFF_K + D_ATTN
OFF_MQ = OFF_V + D_ATTN
OFF_G = OFF_MQ + D_MEM

TOKEN_TILE = 256
MOBA_HEADS_PER_STEP = 2
ROUTE_TILE = 1024
WBUILD_TILE = 256
PEER_TOKEN_TILE = 1024
PEER_EXPERT_CHUNK = 2048
PEER_EXPERT_SUB = 512


def _nt_dot(a, b, precision=None):
    return lax.dot_general(a, b, (((1,), (1,)), ((), ())), precision=precision,
                           preferred_element_type=F32)


def _params(n_grid_dims):
    return pltpu.CompilerParams(dimension_semantics=("arbitrary",) * n_grid_dims,
                                vmem_limit_bytes=VMEM_LIMIT_BYTES)


def _rms(x, gain):
    r = lax.rsqrt(jnp.mean(x * x, axis=-1, keepdims=True) + EPS)
    return x * r * gain


def _resident(shape):
    nd = len(shape)
    return pl.BlockSpec(shape, lambda *_: (0,) * nd, pipeline_mode=pl.Buffered(1))


def _inproj_body(x_ref, an_ref, w_ref, qn_ref, kn_ref, mqn_ref, cw_ref, s0_ref, s1_ref,
                 cy_ref, u_ref, q_ref, k_ref, v_ref, mq_ref, gate_ref, kmean_ref,
                 ubuf_ref, *, tiles_per_seq, seq_len):
    tm = x_ref.shape[0]
    xn = _rms(x_ref[...], an_ref[...]).astype(BF16)

    def proj(off, width):
        return jnp.dot(xn, w_ref[:, off:off + width], preferred_element_type=F32)

    cb = proj(OFF_CB, D_CONV)
    u = proj(OFF_CC, D_CONV) * proj(OFF_CH, D_CONV)
    u_ref[...] = u
    if seq_len >= tm:
        @pl.when(pl.program_id(0) % tiles_per_seq == 0)
        def _():
            ubuf_ref[0:SUBLANES, :] = jnp.zeros((SUBLANES, D_CONV), F32)
    else:
        ubuf_ref[0:SUBLANES, :] = jnp.zeros((SUBLANES, D_CONV), F32)
    ubuf_ref[SUBLANES:SUBLANES + tm, :] = u
    um1 = ubuf_ref[SUBLANES - 1:SUBLANES - 1 + tm, :]
    um2 = ubuf_ref[SUBLANES - 2:SUBLANES - 2 + tm, :]
    if seq_len >= tm:
        ubuf_ref[0:SUBLANES, :] = ubuf_ref[tm:tm + SUBLANES, :]
    else:
        assert seq_len == SUBLANES
        tl = lax.broadcasted_iota(jnp.int32, (tm, D_CONV), 0) % seq_len
        s0 = s0_ref[...]
        s1 = s1_ref[...]
        um1 = jnp.where(tl == 0, s1, um1)
        um2 = jnp.where(tl == 0, s0, jnp.where(tl == 1, s1, um2))
    cw = cw_ref[...]
    y = cw[0:1, :] * um2 + cw[1:2, :] * um1 + cw[2:3, :] * u
    cy_ref[...] = (cb * y).astype(BF16)

    qf = proj(OFF_Q, D_ATTN)
    kf = proj(OFF_K, D_ATTN)
    qn = qn_ref[...]
    kn = kn_ref[...]
    for h in range(N_HEADS):
        sl = slice(h * HEAD_DIM, (h + 1) * HEAD_DIM)
        q_ref[:, sl] = _rms(qf[:, sl], qn)
        k_ref[:, sl] = _rms(kf[:, sl], kn)
    kmean_ref[0] = jnp.mean(k_ref[...], axis=0, keepdims=True)
    v_ref[...] = proj(OFF_V, D_ATTN)
    mqf = proj(OFF_MQ, D_MEM)
    mqn = mqn_ref[...]
    for h in range(MEM_HEADS):
        sl = slice(h * MEM_HEAD_DIM, (h + 1) * MEM_HEAD_DIM)
        mq_ref[:, sl] = _rms(mqf[:, sl], mqn)
    gate_ref[...] = proj(OFF_G, 3 * D_MODEL)


def _inproj(x2, attn_norm, w_in, q_norm, k_norm, mq_norm, conv_w, s0x, s1x, *, seq_len):
    n_tok = x2.shape[0]
    tm = min(TOKEN_TILE, n_tok)
    assert n_tok % tm == 0 and (seq_len % tm == 0 or seq_len == SUBLANES)
    n_tiles = n_tok // tm
    tiles_per_seq = max(seq_len // tm, 1)
    long_seq = seq_len >= tm

    def rows(width):
        return pl.BlockSpec((tm, width), lambda i: (i, 0))

    state_spec = _resident(s0x.shape) if long_seq else rows(D_CONV)
    out_shape = (
        jax.ShapeDtypeStruct((n_tok, D_CONV), BF16),
        jax.ShapeDtypeStruct((n_tok, D_CONV), F32),
        jax.ShapeDtypeStruct((n_tok, D_ATTN), F32),
        jax.ShapeDtypeStruct((n_tok, D_ATTN), F32),
        jax.ShapeDtypeStruct((n_tok, D_ATTN), F32),
        jax.ShapeDtypeStruct((n_tok, D_MEM), F32),
        jax.ShapeDtypeStruct((n_tok, 3 * D_MODEL), F32),
        jax.ShapeDtypeStruct((n_tiles, 1, D_ATTN), F32),
    )
    return pl.pallas_call(
        functools.partial(_inproj_body, tiles_per_seq=tiles_per_seq, seq_len=seq_len),
        grid=(n_tiles,),
        in_specs=[rows(D_MODEL), _resident((1, D_MODEL)), _resident((D_MODEL, D_IN)),
                  _resident((1, HEAD_DIM)), _resident((1, HEAD_DIM)), _resident((1, MEM_HEAD_DIM)),
                  _resident((CONV_W, D_CONV)), state_spec, state_spec],
        out_specs=(rows(D_CONV), rows(D_CONV), rows(D_ATTN), rows(D_ATTN), rows(D_ATTN),
                   rows(D_MEM), rows(3 * D_MODEL),
                   pl.BlockSpec((1, 1, D_ATTN), lambda i: (i, 0, 0))),
        out_shape=out_shape,
        scratch_shapes=[pltpu.VMEM((tm + 2 * SUBLANES, D_CONV), F32)],
        compiler_params=_params(1),
        name="inproj",
    )(x2, attn_norm, w_in, q_norm, k_norm, mq_norm, conv_w, s0x, s1x)


def _moba_prompt_body(q_ref, k_ref, v_ref, km_ref, o_ref, kb_ref, vt_ref, sc_ref, m_ref,
                      *, n_blocks):
    s_id = pl.program_id(2)
    blk = MOBA_BLOCK
    scale = HEAD_DIM ** -0.5
    heads = kb_ref.shape[0]

    @pl.when(s_id == 0)
    def _():
        for hh in range(heads):
            lanes = slice(hh * HEAD_DIM, (hh + 1) * HEAD_DIM)
            kb_ref[hh] = k_ref[:, lanes].astype(BF16)
            for j in range(n_blocks):
                vt_ref[hh, :, j * blk:(j + 1) * blk] = v_ref[j * blk:(j + 1) * blk, lanes].T.astype(BF16)

    key_row = lax.broadcasted_iota(jnp.int32, (blk, blk), 0)
    query_col = lax.broadcasted_iota(jnp.int32, (blk, blk), 1)

    def score_block(own, hh):
        lanes = slice(hh * HEAD_DIM, (hh + 1) * HEAD_DIM)
        q = q_ref[:, lanes]
        gate = _nt_dot(km_ref[0, :, lanes], q, precision=HIGHEST)
        blk_id = lax.broadcasted_iota(jnp.int32, gate.shape, 0)
        rank = jnp.zeros(gate.shape, F32)
        for jp in range(own):
            other = gate[jp:jp + 1, :]
            beats = (other > gate) | ((other == gate) & (jp < blk_id))
            rank = rank + jnp.where(beats, 1.0, 0.0)
        k_sel = min(MOBA_TOPK, n_blocks)
        sel = jnp.where((rank < k_sel) & (blk_id < own), 1.0, 0.0)
        qt = (q * (scale * LOG2_E)).T.astype(BF16)
        buf = own % 2
        m = None
        for j in range(own + 1):
            s = jnp.dot(kb_ref[hh, j * blk:(j + 1) * blk, :], qt, preferred_element_type=F32)
            if j == own:
                s = jnp.where(key_row <= query_col, s, NEG_INF)
            else:
                s = jnp.where(sel[j:j + 1, :] > 0.5, s, NEG_INF)
            sc_ref[hh, buf, j] = s
            mj = jnp.max(s, axis=0, keepdims=True)
            m = mj if m is None else jnp.maximum(m, mj)
        m_ref[hh, buf] = m

    def finish_block(own, hh):
        buf = own % 2
        m = m_ref[hh, buf]
        l = jnp.zeros((1, blk), F32)
        acc = jnp.zeros((HEAD_DIM, blk), F32)
        for j in range(own + 1):
            p = jnp.exp2(sc_ref[hh, buf, j] - m)
            l = l + jnp.sum(p, axis=0, keepdims=True)
            acc = acc + jnp.dot(vt_ref[hh, :, j * blk:(j + 1) * blk], p.astype(BF16),
                                preferred_element_type=F32)
        o_ref[:, hh * HEAD_DIM:(hh + 1) * HEAD_DIM] = (acc / l).T.astype(o_ref.dtype)

    for step in range(n_blocks + 1):
        @pl.when(s_id == step)
        def _(step=step):
            for hh in range(heads):
                if step < n_blocks:
                    score_block(step, hh)
                if step >= 1:
                    finish_block(step - 1, hh)


def _moba_prompt(q2, k2, v2, kmean, *, n_seq, seq_len):
    blk = MOBA_BLOCK
    assert seq_len % blk == 0 and TOKEN_TILE == blk
    nb = seq_len // blk
    km = kmean.reshape(n_seq, nb, D_ATTN)
    hs = MOBA_HEADS_PER_STEP
    width = hs * HEAD_DIM
    assert N_HEADS % hs == 0
    kv_spec = pl.BlockSpec((seq_len, width), lambda n, h, s: (n, h))
    return pl.pallas_call(
        functools.partial(_moba_prompt_body, n_blocks=nb),
        grid=(n_seq, N_HEADS // hs, nb + 1),
        in_specs=[pl.BlockSpec((blk, width), lambda n, h, s: (n * nb + jnp.minimum(s, nb - 1), h)),
                  kv_spec, kv_spec,
                  pl.BlockSpec((1, nb, width), lambda n, h, s: (n, 0, h))],
        out_specs=pl.BlockSpec((blk, width), lambda n, h, s: (n * nb + jnp.maximum(s - 1, 0), h)),
        out_shape=jax.ShapeDtypeStruct(q2.shape, BF16),
        scratch_shapes=[pltpu.VMEM((hs, seq_len, HEAD_DIM), BF16),
                        pltpu.VMEM((hs, HEAD_DIM, seq_len), BF16),
                        pltpu.VMEM((hs, 2, nb, blk, blk), F32), pltpu.VMEM((hs, 2, 1, blk), F32)],
        compiler_params=_params(3),
        name="moba_prompt",
    )(q2, k2, v2, km)


def _moba_sample_body(pt_ref, q_ref, kn_ref, vn_ref, *rest, n_pages, page_size, n_new):
    del pt_ref
    kp = rest[:n_pages]
    vp = rest[n_pages:2 * n_pages]
    o_ref, s_ref = rest[2 * n_pages:]
    rows = n_new * N_HEADS
    page_rows = page_size * N_HEADS
    pages_per_block = MOBA_BLOCK // page_size
    n_past_blocks = n_pages // pages_per_block
    scale = HEAD_DIM ** -0.5

    q = q_ref[...]
    kmeans = []
    for blk in range(n_past_blocks):
        acc = jnp.zeros((N_HEADS, HEAD_DIM), F32)
        for p in range(blk * pages_per_block, (blk + 1) * pages_per_block):
            acc = acc + jnp.sum(kp[p][...], axis=0)
        kmeans.append(acc * (1.0 / MOBA_BLOCK))
    kmean = jnp.concatenate(kmeans, axis=0)
    gate = _nt_dot(q, kmean, precision=HIGHEST)
    r_head = lax.broadcasted_iota(jnp.int32, gate.shape, 0) % N_HEADS
    c_idx = lax.broadcasted_iota(jnp.int32, gate.shape, 1)
    c_blk = c_idx // N_HEADS
    same_head = (c_idx % N_HEADS) == r_head
    rank = jnp.zeros(gate.shape, F32)
    for bp in range(n_past_blocks):
        colval = jnp.max(jnp.where(same_head & (c_blk == bp), gate, NEG_INF), axis=1, keepdims=True)
        beats = (colval > gate) | ((colval == gate) & (bp < c_blk))
        rank = rank + jnp.where(beats, 1.0, 0.0)
    k_sel = min(MOBA_TOPK, n_past_blocks + 1)
    sel = jnp.where(same_head & (rank < k_sel), 1.0, 0.0)

    qb = q.astype(BF16)
    p_head = lax.broadcasted_iota(jnp.int32, (rows, page_rows), 0) % N_HEADS
    p_lane_head = lax.broadcasted_iota(jnp.int32, (rows, page_rows), 1) % N_HEADS
    page_same_head = p_head == p_lane_head
    for blk in range(n_past_blocks):
        blk_sel = jnp.max(jnp.where(c_blk == blk, sel, 0.0), axis=1, keepdims=True) > 0.5
        keep = page_same_head & blk_sel
        for p in range(blk * pages_per_block, (blk + 1) * pages_per_block):
            kpage = kp[p][...].reshape(page_rows, HEAD_DIM).astype(BF16)
            s = _nt_dot(qb, kpage) * scale
            s_ref[:, p * page_rows:(p + 1) * page_rows] = jnp.where(keep, s, NEG_INF)
    pad = jnp.zeros((LANES - rows, HEAD_DIM), F32)
    kn = jnp.concatenate([kn_ref[...], pad], axis=0).astype(BF16)
    vn = jnp.concatenate([vn_ref[...], pad], axis=0).astype(BF16)
    sn = _nt_dot(qb, kn) * scale
    n_row = lax.broadcasted_iota(jnp.int32, sn.shape, 0)
    n_col = lax.broadcasted_iota(jnp.int32, sn.shape, 1)
    keep_new = ((n_col % N_HEADS) == (n_row % N_HEADS)) & (n_col // N_HEADS <= n_row // N_HEADS) \
        & (n_col < rows)
    past = n_pages * page_rows
    s_ref[:, past:past + LANES] = jnp.where(keep_new, sn, NEG_INF)

    m = jnp.max(s_ref[...], axis=1, keepdims=True)
    l = jnp.zeros((rows, 1), F32)
    acc = jnp.zeros((rows, HEAD_DIM), F32)
    for p in range(n_pages):
        pp = jnp.exp(s_ref[:, p * page_rows:(p + 1) * page_rows] - m)
        l = l + jnp.sum(pp, axis=1, keepdims=True)
        vpage = vp[p][...].reshape(page_rows, HEAD_DIM).astype(BF16)
        acc = acc + jnp.dot(pp.astype(BF16), vpage, preferred_element_type=F32)
    pn = jnp.exp(s_ref[:, past:past + LANES] - m)
    l = l + jnp.sum(pn, axis=1, keepdims=True)
    acc = acc + jnp.dot(pn.astype(BF16), vn, preferred_element_type=F32)
    o_ref[...] = (acc / l).astype(o_ref.dtype)


def _moba_sample(q8, kn8, vn8, cache_k, cache_v, page_table, layer, *, n_seq, n_new, past_len):
    page_size = cache_k.shape[2]
    n_pages = page_table.shape[1]
    assert past_len == n_pages * page_size and past_len % MOBA_BLOCK == 0
    assert MOBA_BLOCK % page_size == 0 and n_new <= MOBA_BLOCK and n_new * N_HEADS <= LANES
    assert cache_k.shape[3:] == (N_HEADS, HEAD_DIM) and N_HEADS == SUBLANES
    rows = n_new * N_HEADS
    row_spec = pl.BlockSpec((rows, HEAD_DIM), lambda s, pt: (s, 0))

    def page_spec(p):
        return pl.BlockSpec((None, None, page_size, N_HEADS, HEAD_DIM),
                            lambda s, pt: (layer, pt[s, p], 0, 0, 0))

    page_specs = [page_spec(p) for p in range(n_pages)]
    grid_spec = pltpu.PrefetchScalarGridSpec(
        num_scalar_prefetch=1,
        grid=(n_seq,),
        in_specs=[row_spec, row_spec, row_spec] + page_specs + page_specs,
        out_specs=row_spec,
        scratch_shapes=[pltpu.VMEM((rows, n_pages * page_size * N_HEADS + LANES), F32)],
    )
    return pl.pallas_call(
        functools.partial(_moba_sample_body, n_pages=n_pages, page_size=page_size, n_new=n_new),
        grid_spec=grid_spec,
        out_shape=jax.ShapeDtypeStruct(q8.shape, BF16),
        compiler_params=_params(1),
        name="moba_sample",
    )(page_table, q8, kn8, vn8, *([cache_k] * n_pages), *([cache_v] * n_pages))


def _memkv_body(mem_ref, mn_ref, w_ref, mkn_ref, k_ref, v_ref):
    xn = _rms(mem_ref[...], mn_ref[...]).astype(BF16)
    kv = jnp.dot(xn, w_ref[...], preferred_element_type=F32)
    mkn = mkn_ref[...]
    for h in range(MEM_HEADS):
        sl = slice(h * MEM_HEAD_DIM, (h + 1) * MEM_HEAD_DIM)
        k_ref[:, sl] = _rms(kv[:, sl], mkn)
    v_ref[...] = kv[:, D_MEM:]


def _memkv(mem2, mem_norm, w_mem_kv, mk_norm):
    n_rows = mem2.shape[0]
    tm = N_MEM
    spec = pl.BlockSpec((tm, D_MEM), lambda i: (i, 0))
    return pl.pallas_call(
        _memkv_body,
        grid=(n_rows // tm,),
        in_specs=[pl.BlockSpec((tm, D_MODEL), lambda i: (i, 0)), _resident((1, D_MODEL)),
                  _resident((D_MODEL, 2 * D_MEM)), _resident((1, MEM_HEAD_DIM))],
        out_specs=(spec, spec),
        out_shape=(jax.ShapeDtypeStruct((n_rows, D_MEM), F32),) * 2,
        compiler_params=_params(1),
        name="memkv",
    )(mem2, mem_norm, w_mem_kv, mk_norm)


def _memattn_body(mq_ref, k_ref, v_ref, o_ref, *, groups, rows, heads_on_lanes):
    scale = MEM_HEAD_DIM ** -0.5
    for g in range(groups):
        rs = slice(g * rows, (g + 1) * rows)
        for h in range(MEM_HEADS):
            sl = slice(h * MEM_HEAD_DIM, (h + 1) * MEM_HEAD_DIM)
            if heads_on_lanes:
                kh, vh = k_ref[g, :, sl], v_ref[g, :, sl]
            else:
                kh, vh = k_ref[g, :, h, :], v_ref[g, :, h, :]
            qh = mq_ref[rs, sl].astype(BF16)
            s = _nt_dot(qh, kh.astype(BF16)) * scale
            p = jnp.exp(s - jnp.max(s, axis=1, keepdims=True))
            l = jnp.sum(p, axis=1, keepdims=True)
            o = jnp.dot(p.astype(BF16), vh.astype(BF16), preferred_element_type=F32)
            o_ref[rs, sl] = (o / l).astype(o_ref.dtype)


def _memattn(mq2, mem_k, mem_v, mem_spec, *, groups, rows):
    n_tok = mq2.shape[0]
    tm = groups * rows
    row_spec = pl.BlockSpec((tm, D_MEM), lambda i: (i, 0))
    heads_on_lanes = sum(d is not None for d in mem_spec.block_shape) == 3
    return pl.pallas_call(
        functools.partial(_memattn_body, groups=groups, rows=rows, heads_on_lanes=heads_on_lanes),
        grid=(n_tok // tm,),
        in_specs=[row_spec, mem_spec, mem_spec],
        out_specs=row_spec,
        out_shape=jax.ShapeDtypeStruct((n_tok, D_MEM), BF16),
        compiler_params=_params(1),
        name="memattn",
    )(mq2, mem_k, mem_v)


def _post_body(x_ref, cy_ref, o_ref, om_ref, gate_ref, wco_ref, wao_ref, wmo_ref, wo_ref,
               fn_ref, wq_ref, h_ref, xn_ref, qh_ref):
    conv_out = jnp.dot(cy_ref[...], wco_ref[...], preferred_element_type=F32)
    attn_out = jnp.dot(o_ref[...], wao_ref[...], preferred_element_type=F32)
    mem_out = jnp.dot(om_ref[...], wmo_ref[...], preferred_element_type=F32)
    merged = (jax.nn.sigmoid(gate_ref[:, 0:D_MODEL]) * conv_out
              + jax.nn.sigmoid(gate_ref[:, D_MODEL:2 * D_MODEL]) * attn_out
              + jax.nn.sigmoid(gate_ref[:, 2 * D_MODEL:3 * D_MODEL]) * mem_out)
    h = x_ref[...] + jnp.dot(merged.astype(BF16), wo_ref[...], preferred_element_type=F32)
    h_ref[...] = h
    xn = _rms(h, fn_ref[...]).astype(BF16)
    xn_ref[...] = xn
    qh = jnp.dot(xn, wq_ref[...], preferred_element_type=F32)
    for hc in range(2 * PEER_HEADS):
        qh_ref[hc] = qh[:, hc * PEER_DKEY:(hc + 1) * PEER_DKEY].astype(BF16)


def _post(x2, cy, o2, om, gates, wco, wao, wmo, wo, ffn_norm, wq):
    n_tok = x2.shape[0]
    tm = min(TOKEN_TILE, n_tok)

    def rows(width):
        return pl.BlockSpec((tm, width), lambda i: (i, 0))

    return pl.pallas_call(
        _post_body,
        grid=(n_tok // tm,),
        in_specs=[rows(D_MODEL), rows(D_CONV), rows(D_ATTN), rows(D_MEM), rows(3 * D_MODEL),
                  _resident(wco.shape), _resident(wao.shape), _resident(wmo.shape),
                  _resident(wo.shape), _resident((1, D_MODEL)), _resident(wq.shape)],
        out_specs=(rows(D_MODEL), rows(D_MODEL),
                   pl.BlockSpec((2 * PEER_HEADS, tm, PEER_DKEY), lambda i: (0, i, 0))),
        out_shape=(jax.ShapeDtypeStruct((n_tok, D_MODEL), F32),
                   jax.ShapeDtypeStruct((n_tok, D_MODEL), BF16),
                   jax.ShapeDtypeStruct((2 * PEER_HEADS, n_tok, PEER_DKEY), BF16)),
        compiler_params=_params(1),
        name="post",
    )(x2, cy, o2, om, gates, wco, wao, wmo, wo, ffn_norm, wq)


_CAND_PAIRS = tuple((a, b) for a in range(PEER_TOPK) for b in range(PEER_TOPK)
                    if (a + 1) * (b + 1) <= PEER_TOPK)


def _tree_argmax(val, idx):
    chunk = 2 * SUBLANES
    if val.shape[0] > chunk:
        parts = [_tree_argmax(val[c:c + chunk], idx[c:c + chunk])
                 for c in range(0, val.shape[0], chunk)]
        val = jnp.stack([p[0] for p in parts])
        idx = jnp.stack([p[1] for p in parts])
    while val.shape[0] > 1:
        n = val.shape[0]
        pairs = n // 2
        v2 = val[:2 * pairs].reshape((pairs, 2) + val.shape[1:])
        i2 = idx[:2 * pairs].reshape((pairs, 2) + idx.shape[1:])
        left = v2[:, 0] >= v2[:, 1]
        new_val = jnp.maximum(v2[:, 0], v2[:, 1])
        new_idx = jnp.where(left, i2[:, 0], i2[:, 1])
        if n % 2:
            new_val = jnp.concatenate([new_val, val[2 * pairs:]], axis=0)
            new_idx = jnp.concatenate([new_idx, idx[2 * pairs:]], axis=0)
        val, idx = new_val, new_idx
    return val[0], idx[0]


def _route_scratch(tm):
    tile = (tm // LANES, LANES)
    keys = pltpu.VMEM((PEER_NKEYS,) + tile, F32)
    small = pltpu.VMEM((PEER_TOPK,) + tile, F32)
    slots = pltpu.VMEM((PEER_HEADS * PEER_TOPK,) + tile, F32)
    return [keys, keys, small, small, small, small, small, small, slots, slots, slots]


def _route_head(h, qh1_ref, qh2_ref, sub1_ref, sub2_ref, scratch):
    (x1_ref, x2_ref, st1_ref, ix1_ref, st2_ref, ix2_ref, best_ref, flat_ref,
     si_ref, sj_ref, sg_ref) = scratch
    n_sub = x1_ref.shape[1]
    k = PEER_TOPK
    tile = (n_sub, LANES)
    key_id = lax.broadcasted_iota(jnp.int32, (PEER_NKEYS,) + tile, 0).astype(F32)
    flat_id = jnp.stack([jnp.full(tile, float(a * k + b), F32) for a, b in _CAND_PAIRS])

    def scores_into(sub_ref, qh_ref, x_ref):
        for s in range(n_sub):
            x_ref[:, s, :] = _nt_dot(sub_ref[...], qh_ref[s * LANES:(s + 1) * LANES, :])

    def pop_max(x_ref, r, st_ref, ix_ref):
        x = x_ref[...]
        m, f = _tree_argmax(x, key_id)
        st_ref[r] = m
        ix_ref[r] = f
        x_ref[...] = jnp.where(key_id == f[None], NEG_INF, x)

    def lookup(rank, table):
        out = jnp.zeros(rank.shape, F32)
        for a in range(k):
            out = jnp.where(rank == float(a), table[a][None], out)
        return out

    scores_into(sub1_ref, qh1_ref, x1_ref)
    scores_into(sub2_ref, qh2_ref, x2_ref)

    def rnd1(r, c):
        pop_max(x1_ref, r, st1_ref, ix1_ref)
        pop_max(x2_ref, r, st2_ref, ix2_ref)
        return c

    lax.fori_loop(0, k, rnd1, 0)
    st1 = st1_ref[...]
    st2 = st2_ref[...]
    cand = jnp.stack([st1[a] + st2[b] for a, b in _CAND_PAIRS])

    def rnd2(r, x):
        m, f = _tree_argmax(x, flat_id)
        best_ref[r] = m
        flat_ref[r] = f
        return jnp.where(flat_id == f[None], NEG_INF, x)

    lax.fori_loop(0, k, rnd2, cand)
    best = best_ref[...]
    flat = flat_ref[...]
    rank_a = jnp.floor(flat * (1.0 / k))
    rank_b = flat - rank_a * k
    e = jnp.exp(best - best[0:1])
    rows = pl.ds(pl.multiple_of(h * k, k), k)
    si_ref[rows] = lookup(rank_a, ix1_ref[...])
    sj_ref[rows] = lookup(rank_b, ix2_ref[...])
    sg_ref[rows] = e / jnp.sum(e, axis=0, keepdims=True)


def _route_emit(scratch, i_ref, j_ref, g_ref):
    si_ref, sj_ref, sg_ref = scratch[-3:]
    for s in range(si_ref.shape[1]):
        rows = slice(s * LANES, (s + 1) * LANES)
        i_ref[rows, :] = si_ref[:, s, :].T
        j_ref[rows, :] = sj_ref[:, s, :].T
        g_ref[rows, :] = sg_ref[:, s, :].T


def _route_body(qh_ref, sub_ref, i_ref, j_ref, g_ref, *scratch):
    def head(h, carry):
        _route_head(h, qh_ref.at[2 * h], qh_ref.at[2 * h + 1], sub_ref.at[2 * h],
                    sub_ref.at[2 * h + 1], scratch)
        return carry

    lax.fori_loop(0, PEER_HEADS, head, 0)
    _route_emit(scratch, i_ref, j_ref, g_ref)


def _route(qh3, subkeys):
    n_tok = qh3.shape[1]
    tm = min(ROUTE_TILE, n_tok)
    n_slots = PEER_HEADS * PEER_TOPK
    assert n_slots == LANES and tm % LANES == 0 and n_tok % tm == 0
    sub3 = subkeys.reshape(2 * PEER_HEADS, PEER_NKEYS, PEER_DKEY).astype(BF16)
    slot_spec = pl.BlockSpec((tm, n_slots), lambda i: (i, 0))
    return pl.pallas_call(
        _route_body,
        grid=(n_tok // tm,),
        in_specs=[pl.BlockSpec((2 * PEER_HEADS, tm, PEER_DKEY), lambda i: (0, i, 0)),
                  _resident(sub3.shape)],
        out_specs=(slot_spec,) * 3,
        out_shape=(jax.ShapeDtypeStruct((n_tok, n_slots), F32),) * 3,
        scratch_shapes=_route_scratch(tm),
        compiler_params=_params(1),
        name="peer_route",
    )(qh3, sub3)


WBUILD_GROUP = 16
WBUILD_PITCH = PEER_NKEYS + SUBLANES


def _wbuild_body(i_ref, j_ref, g_ref, w_ref, stage_a_ref, stage_b_ref):
    tm = i_ref.shape[0]
    key = lax.broadcasted_iota(jnp.int32, (PEER_NKEYS, LANES), 0).astype(F32)

    def build(gi, stage_ref):
        base = pl.multiple_of(gi * WBUILD_GROUP, WBUILD_GROUP)
        for tt in range(WBUILD_GROUP):
            irow = i_ref[pl.ds(base + tt, 1), :]
            jrow = j_ref[pl.ds(base + tt, 1), :]
            grow = g_ref[pl.ds(base + tt, 1), :]
            a = jnp.where(key == irow, grow, 0.0).astype(BF16)
            b = jnp.where(key == jrow, 1.0, 0.0).astype(BF16)
            stage_ref[tt * WBUILD_PITCH:tt * WBUILD_PITCH + PEER_NKEYS, :] = _nt_dot(a, b)

    def drain(gi, stage_ref):
        base = pl.multiple_of(gi * WBUILD_GROUP, WBUILD_GROUP)
        for i in range(PEER_NKEYS):
            tile = stage_ref[pl.ds(i, WBUILD_GROUP, stride=WBUILD_PITCH), :]
            w_ref[pl.ds(base, WBUILD_GROUP), i * PEER_NKEYS:(i + 1) * PEER_NKEYS] = tile.astype(BF16)

    n_groups = tm // WBUILD_GROUP
    assert n_groups % 2 == 0
    build(0, stage_a_ref)

    def step(p, carry):
        build(2 * p + 1, stage_b_ref)
        drain(2 * p, stage_a_ref)
        build(2 * p + 2, stage_a_ref)
        drain(2 * p + 1, stage_b_ref)
        return carry

    lax.fori_loop(0, n_groups // 2 - 1, step, 0)
    build(n_groups - 1, stage_b_ref)
    drain(n_groups - 2, stage_a_ref)
    drain(n_groups - 1, stage_b_ref)


def _wbuild(si, sj, sg):
    n_tok = si.shape[0]
    tm = min(WBUILD_TILE, n_tok)
    assert tm % WBUILD_GROUP == 0
    slot_spec = pl.BlockSpec((tm, LANES), lambda i: (i, 0))
    return pl.pallas_call(
        _wbuild_body,
        grid=(n_tok // tm,),
        in_specs=[slot_spec] * 3,
        out_specs=pl.BlockSpec((tm, PEER_N_EXPERTS), lambda i: (i, 0)),
        out_shape=jax.ShapeDtypeStruct((n_tok, PEER_N_EXPERTS), BF16),
        scratch_shapes=[pltpu.VMEM((WBUILD_GROUP * WBUILD_PITCH, PEER_NKEYS), F32)] * 2,
        compiler_params=_params(1),
        name="peer_wbuild",
    )(si, sj, sg)


def _peer_body(xn_ref, u_ref, v_ref, w_ref, h_ref, y_ref):
    @pl.when(pl.program_id(1) == 0)
    def _():
        y_ref[...] = h_ref[...]

    xn = xn_ref[...]
    sub = PEER_EXPERT_SUB
    for s in range(u_ref.shape[0] // sub):
        sl = slice(s * sub, (s + 1) * sub)
        z = _nt_dot(xn, u_ref[sl, :])
        a = 0.5 * z * (1.0 + lax.erf(z * (2.0 ** -0.5)))
        a = a * w_ref[:, sl].astype(F32)
        y_ref[...] += jnp.dot(a.astype(BF16), v_ref[sl, :], preferred_element_type=F32)


def _peer_dense(xn, w2, h, peer_u, peer_v):
    n_tok = xn.shape[0]
    tm = min(PEER_TOKEN_TILE, n_tok)
    ec = PEER_EXPERT_CHUNK
    assert n_tok % tm == 0 and PEER_N_EXPERTS % ec == 0 and ec % PEER_EXPERT_SUB == 0
    tok_spec = pl.BlockSpec((tm, D_MODEL), lambda i, c: (i, 0))
    exp_spec = pl.BlockSpec((ec, D_MODEL), lambda i, c: (c, 0))
    return pl.pallas_call(
        _peer_body,
        grid=(n_tok // tm, PEER_N_EXPERTS // ec),
        in_specs=[tok_spec, exp_spec, exp_spec, pl.BlockSpec((tm, ec), lambda i, c: (i, c)),
                  pl.BlockSpec((tm, D_MODEL), lambda i, c: (i, 0), pipeline_mode=pl.Buffered(1))],
        out_specs=tok_spec,
        out_shape=jax.ShapeDtypeStruct((n_tok, D_MODEL), F32),
        compiler_params=_params(2),
        name="peer_dense",
    )(xn, peer_u, peer_v, w2, h)


def _row(v):
    return v.reshape(1, -1)


def _channel_mixer(x2, cy, o2, om, gates, lw):
    h, xn, qh3 = _post(x2, cy, o2, om, gates, lw["wco"], lw["wao"], lw["wmo"], lw["wo"],
                       lw["ffn_norm"], lw["wq"])
    si, sj, sg = _route(qh3, lw["subkeys"])
    w2 = _wbuild(si, sj, sg)
    return _peer_dense(xn, w2, h, lw["peer_u"], lw["peer_v"])


def kernel(x_prompt, x_sample, mem_prompt, cache_k, cache_v, cache_conv, cache_mem_k, cache_mem_v,
           page_table, attn_norm, w_in, conv_w, w_conv_out, q_norm, k_norm, w_attn_out,
           mem_norm, w_mem_kv, mk_norm, mq_norm, w_mem_out, w_o,
           ffn_norm, peer_wq, peer_subkeys, peer_u, peer_v):
    n_p, t_p, _ = x_prompt.shape
    n_s, t_s, _ = x_sample.shape
    depth = attn_norm.shape[0]
    past_len = page_table.shape[1] * cache_k.shape[2]
    h_p = x_prompt.reshape(n_p * t_p, D_MODEL)
    h_s = x_sample.reshape(n_s * t_s, D_MODEL)
    outs = [[] for _ in range(8)]
    for l in range(depth):
        lw = dict(
            wco=w_conv_out[l].astype(BF16), wao=w_attn_out[l].astype(BF16),
            wmo=w_mem_out[l].astype(BF16), wo=w_o[l].astype(BF16), wq=peer_wq[l].astype(BF16),
            ffn_norm=_row(ffn_norm[l]), subkeys=peer_subkeys[l],
            peer_u=peer_u[l].astype(BF16), peer_v=peer_v[l].astype(BF16))
        in_w = (_row(attn_norm[l]), w_in[l].astype(BF16), _row(q_norm[l]), _row(k_norm[l]),
                _row(mq_norm[l]), conv_w[l])

        mem_k, mem_v = _memkv(mem_prompt.reshape(n_p * N_MEM, D_MODEL), _row(mem_norm[l]),
                              w_mem_kv[l].astype(BF16), _row(mk_norm[l]))
        zero_state = jnp.zeros((SUBLANES, D_CONV), F32)
        cy, u, q, k, v, mq, gates, kmean = _inproj(h_p, *in_w, zero_state, zero_state, seq_len=t_p)
        o = _moba_prompt(q, k, v, kmean, n_seq=n_p, seq_len=t_p)
        steps_per_mem = t_p // TOKEN_TILE

        prompt_mem = pl.BlockSpec((1, N_MEM, D_MEM), lambda i: (i // steps_per_mem, 0, 0))
        om = _memattn(mq, mem_k.reshape(n_p, N_MEM, D_MEM), mem_v.reshape(n_p, N_MEM, D_MEM),
                      prompt_mem, groups=1, rows=TOKEN_TILE)
        h_p = _channel_mixer(h_p, cy, o, om, gates, lw)
        outs[0].append(k.reshape(n_p, t_p, N_HEADS, HEAD_DIM))
        outs[1].append(v.reshape(n_p, t_p, N_HEADS, HEAD_DIM))
        outs[2].append(u.reshape(n_p, t_p, D_CONV)[:, t_p - (CONV_W - 1):])
        outs[3].append(mem_k.reshape(n_p, N_MEM, MEM_HEADS, MEM_HEAD_DIM))
        outs[4].append(mem_v.reshape(n_p, N_MEM, MEM_HEADS, MEM_HEAD_DIM))

        state = cache_conv[l]
        s0x = jnp.repeat(state[:, 0], t_s, axis=0)
        s1x = jnp.repeat(state[:, 1], t_s, axis=0)
        cy, u, q, k, v, mq, gates, _ = _inproj(h_s, *in_w, s0x, s1x, seq_len=t_s)
        k4 = k.reshape(n_s, t_s, N_HEADS, HEAD_DIM)
        v4 = v.reshape(n_s, t_s, N_HEADS, HEAD_DIM)
        rows = n_s * t_s * N_HEADS
        o = _moba_sample(q.reshape(rows, HEAD_DIM), k4.reshape(rows, HEAD_DIM),
                         v4.reshape(rows, HEAD_DIM), cache_k, cache_v, page_table, l,
                         n_seq=n_s, n_new=t_s, past_len=past_len)
        o = o.reshape(n_s * t_s, D_ATTN)
        groups = SUBLANES

        sample_mem = pl.BlockSpec((None, groups, N_MEM, MEM_HEADS, MEM_HEAD_DIM),
                                  lambda i: (l, i, 0, 0, 0))
        om = _memattn(mq, cache_mem_k, cache_mem_v, sample_mem, groups=groups, rows=t_s)
        h_s = _channel_mixer(h_s, cy, o, om, gates, lw)
        outs[5].append(k4)
        outs[6].append(v4)
        ext = jnp.concatenate([state, u.reshape(n_s, t_s, D_CONV)], axis=1)
        outs[7].append(ext[:, t_s:])
    stacked = [jnp.stack(o) for o in outs]
    return (h_p.reshape(n_p, t_p, D_MODEL), h_s.reshape(n_s, t_s, D_MODEL), *stacked)
```

```python
import functools

import jax
import jax.numpy as jnp
from jax import lax
from jax.experimental import pallas as pl
from jax.experimental.pallas import tpu as pltpu

F32 = jnp.float32
BF16 = jnp.bfloat16
HIGHEST = lax.Precision.HIGHEST
NEG_INF = float("-inf")
LOG2_E = 1.4426950408889634

SUBLANES = 8
LANES = 128
VMEM_LIMIT_BYTES = 56 * 1024 * 1024

EPS = 1e-6
D_MODEL = 1024
D_CONV = 512
CONV_W = 3
N_HEADS = 8
HEAD_DIM = 128
D_ATTN = N_HEADS * HEAD_DIM
MOBA_BLOCK = 256
MOBA_TOPK = 3
N_MEM = 256
MEM_HEADS = 4
MEM_HEAD_DIM = 128
D_MEM = MEM_HEADS * MEM_HEAD_DIM
PEER_HEADS = 8
PEER_NKEYS = 128
PEER_DKEY = 128
PEER_TOPK = 16
PEER_N_EXPERTS = PEER_NKEYS * PEER_NKEYS
D_PEER_Q = PEER_HEADS * 2 * PEER_DKEY
D_IN = 3 * D_CONV + 3 * D_ATTN + D_MEM + 3 * D_MODEL
OFF_CB, OFF_CC, OFF_CH = 0, D_CONV, 2 * D_CONV
OFF_Q = 3 * D_CONV
OFF_K = OFF_Q + D_ATTN
OFF_V = OFF_K + D_ATTN
OFF_MQ = OFF_V + D_ATTN
OFF_G = OFF_MQ + D_MEM

TOKEN_TILE = 256
MOBA_HEADS_PER_STEP = 2
ROUTE_TILE = 1024
WBUILD_TILE = 512
PEER_TOKEN_TILE = 1024
PEER_EXPERT_CHUNK = 2048
PEER_EXPERT_SUB = 512


def _nt_dot(a, b, precision=None):
    return lax.dot_general(a, b, (((1,), (1,)), ((), ())), precision=precision,
                           preferred_element_type=F32)


def _params(n_grid_dims):
    return pltpu.CompilerParams(dimension_semantics=("arbitrary",) * n_grid_dims,
                                vmem_limit_bytes=VMEM_LIMIT_BYTES)


def _rms(x, gain):
    r = lax.rsqrt(jnp.mean(x * x, axis=-1, keepdims=True) + EPS)
    return x * r * gain


def _resident(shape):
    nd = len(shape)
    return pl.BlockSpec(shape, lambda *_: (0,) * nd, pipeline_mode=pl.Buffered(1))


def _inproj_body(x_ref, an_ref, w_ref, qn_ref, kn_ref, mqn_ref, cw_ref, s0_ref, s1_ref,
                 cy_ref, u_ref, q_ref, k_ref, v_ref, mq_ref, gate_ref, kmean_ref,
                 ubuf_ref, *, tiles_per_seq, seq_len):
    tm = x_ref.shape[0]
    xn = _rms(x_ref[...], an_ref[...]).astype(BF16)

    def proj(off, width):
        return jnp.dot(xn, w_ref[:, off:off + width], preferred_element_type=F32)

    cb = proj(OFF_CB, D_CONV)
    u = proj(OFF_CC, D_CONV) * proj(OFF_CH, D_CONV)
    u_ref[...] = u
    if seq_len >= tm:
        @pl.when(pl.program_id(0) % tiles_per_seq == 0)
        def _():
            ubuf_ref[0:SUBLANES, :] = jnp.zeros((SUBLANES, D_CONV), F32)
    else:
        ubuf_ref[0:SUBLANES, :] = jnp.zeros((SUBLANES, D_CONV), F32)
    ubuf_ref[SUBLANES:SUBLANES + tm, :] = u
    um1 = ubuf_ref[SUBLANES - 1:SUBLANES - 1 + tm, :]
    um2 = ubuf_ref[SUBLANES - 2:SUBLANES - 2 + tm, :]
    if seq_len >= tm:
        ubuf_ref[0:SUBLANES, :] = ubuf_ref[tm:tm + SUBLANES, :]
    else:
        assert seq_len == SUBLANES
        tl = lax.broadcasted_iota(jnp.int32, (tm, D_CONV), 0) % seq_len
        s0 = s0_ref[...]
        s1 = s1_ref[...]
        um1 = jnp.where(tl == 0, s1, um1)
        um2 = jnp.where(tl == 0, s0, jnp.where(tl == 1, s1, um2))
    cw = cw_ref[...]
    y = cw[0:1, :] * um2 + cw[1:2, :] * um1 + cw[2:3, :] * u
    cy_ref[...] = (cb * y).astype(BF16)

    qf = proj(OFF_Q, D_ATTN)
    kf = proj(OFF_K, D_ATTN)
    qn = qn_ref[...]
    kn = kn_ref[...]
    for h in range(N_HEADS):
        sl = slice(h * HEAD_DIM, (h + 1) * HEAD_DIM)
        q_ref[:, sl] = _rms(qf[:, sl], qn)
        k_ref[:, sl] = _rms(kf[:, sl], kn)
    kmean_ref[0] = jnp.mean(k_ref[...], axis=0, keepdims=True)
    v_ref[...] = proj(OFF_V, D_ATTN)
    mqf = proj(OFF_MQ, D_MEM)
    mqn = mqn_ref[...]
    for h in range(MEM_HEADS):
        sl = slice(h * MEM_HEAD_DIM, (h + 1) * MEM_HEAD_DIM)
        mq_ref[:, sl] = _rms(mqf[:, sl], mqn)
    gate_ref[...] = proj(OFF_G, 3 * D_MODEL)


def _inproj(x2, attn_norm, w_in, q_norm, k_norm, mq_norm, conv_w, s0x, s1x, *, seq_len):
    n_tok = x2.shape[0]
    tm = min(TOKEN_TILE, n_tok)
    assert n_tok % tm == 0 and (seq_len % tm == 0 or seq_len == SUBLANES)
    n_tiles = n_tok // tm
    tiles_per_seq = max(seq_len // tm, 1)
    long_seq = seq_len >= tm

    def rows(width):
        return pl.BlockSpec((tm, width), lambda i: (i, 0))

    state_spec = _resident(s0x.shape) if long_seq else rows(D_CONV)
    out_shape = (
        jax.ShapeDtypeStruct((n_tok, D_CONV), BF16),
        jax.ShapeDtypeStruct((n_tok, D_CONV), F32),
        jax.ShapeDtypeStruct((n_tok, D_ATTN), F32),
        jax.ShapeDtypeStruct((n_tok, D_ATTN), F32),
        jax.ShapeDtypeStruct((n_tok, D_ATTN), F32),
        jax.ShapeDtypeStruct((n_tok, D_MEM), F32),
        jax.ShapeDtypeStruct((n_tok, 3 * D_MODEL), F32),
        jax.ShapeDtypeStruct((n_tiles, 1, D_ATTN), F32),
    )
    return pl.pallas_call(
        functools.partial(_inproj_body, tiles_per_seq=tiles_per_seq, seq_len=seq_len),
        grid=(n_tiles,),
        in_specs=[rows(D_MODEL), _resident((1, D_MODEL)), _resident((D_MODEL, D_IN)),
                  _resident((1, HEAD_DIM)), _resident((1, HEAD_DIM)), _resident((1, MEM_HEAD_DIM)),
                  _resident((CONV_W, D_CONV)), state_spec, state_spec],
        out_specs=(rows(D_CONV), rows(D_CONV), rows(D_ATTN), rows(D_ATTN), rows(D_ATTN),
                   rows(D_MEM), rows(3 * D_MODEL),
                   pl.BlockSpec((1, 1, D_ATTN), lambda i: (i, 0, 0))),
        out_shape=out_shape,
        scratch_shapes=[pltpu.VMEM((tm + 2 * SUBLANES, D_CONV), F32)],
        compiler_params=_params(1),
        name="inproj",
    )(x2, attn_norm, w_in, q_norm, k_norm, mq_norm, conv_w, s0x, s1x)


def _moba_prompt_body(q_ref, k_ref, v_ref, km_ref, o_ref, kb_ref, vt_ref, sc_ref, m_ref,
                      *, n_blocks):
    s_id = pl.program_id(2)
    blk = MOBA_BLOCK
    scale = HEAD_DIM ** -0.5
    heads = kb_ref.shape[0]

    @pl.when(s_id == 0)
    def _():
        for hh in range(heads):
            lanes = slice(hh * HEAD_DIM, (hh + 1) * HEAD_DIM)
            kb_ref[hh] = k_ref[:, lanes].astype(BF16)
            for j in range(n_blocks):
                vt_ref[hh, :, j * blk:(j + 1) * blk] = v_ref[j * blk:(j + 1) * blk, lanes].T.astype(BF16)

    key_row = lax.broadcasted_iota(jnp.int32, (blk, blk), 0)
    query_col = lax.broadcasted_iota(jnp.int32, (blk, blk), 1)

    def score_block(own, hh):
        lanes = slice(hh * HEAD_DIM, (hh + 1) * HEAD_DIM)
        q = q_ref[:, lanes]
        gate = _nt_dot(km_ref[0, :, lanes], q, precision=HIGHEST)
        blk_id = lax.broadcasted_iota(jnp.int32, gate.shape, 0)
        rank = jnp.zeros(gate.shape, F32)
        for jp in range(own):
            other = gate[jp:jp + 1, :]
            beats = (other > gate) | ((other == gate) & (jp < blk_id))
            rank = rank + jnp.where(beats, 1.0, 0.0)
        k_sel = min(MOBA_TOPK, n_blocks)
        sel = jnp.where((rank < k_sel) & (blk_id < own), 1.0, 0.0)
        qt = (q * (scale * LOG2_E)).T.astype(BF16)
        buf = own % 2
        m = None
        for j in range(own + 1):
            s = jnp.dot(kb_ref[hh, j * blk:(j + 1) * blk, :], qt, preferred_element_type=F32)
            if j == own:
                s = jnp.where(key_row <= query_col, s, NEG_INF)
            else:
                s = jnp.where(sel[j:j + 1, :] > 0.5, s, NEG_INF)
            sc_ref[hh, buf, j] = s
            mj = jnp.max(s, axis=0, keepdims=True)
            m = mj if m is None else jnp.maximum(m, mj)
        m_ref[hh, buf] = m

    def finish_block(own, hh):
        buf = own % 2
        m = m_ref[hh, buf]
        l = jnp.zeros((1, blk), F32)
        acc = jnp.zeros((HEAD_DIM, blk), F32)
        for j in range(own + 1):
            p = jnp.exp2(sc_ref[hh, buf, j] - m)
            l = l + jnp.sum(p, axis=0, keepdims=True)
            acc = acc + jnp.dot(vt_ref[hh, :, j * blk:(j + 1) * blk], p.astype(BF16),
                                preferred_element_type=F32)
        o_ref[:, hh * HEAD_DIM:(hh + 1) * HEAD_DIM] = (acc / l).T.astype(o_ref.dtype)

    for step in range(n_blocks + 1):
        @pl.when(s_id == step)
        def _(step=step):
            for hh in range(heads):
                if step < n_blocks:
                    score_block(step, hh)
                if step >= 1:
                    finish_block(step - 1, hh)


def _moba_prompt(q2, k2, v2, kmean, *, n_seq, seq_len):
    blk = MOBA_BLOCK
    assert seq_len % blk == 0 and TOKEN_TILE == blk
    nb = seq_len // blk
    km = kmean.reshape(n_seq, nb, D_ATTN)
    hs = MOBA_HEADS_PER_STEP
    width = hs * HEAD_DIM
    assert N_HEADS % hs == 0
    kv_spec = pl.BlockSpec((seq_len, width), lambda n, h, s: (n, h))
    return pl.pallas_call(
        functools.partial(_moba_prompt_body, n_blocks=nb),
        grid=(n_seq, N_HEADS // hs, nb + 1),
        in_specs=[pl.BlockSpec((blk, width), lambda n, h, s: (n * nb + jnp.minimum(s, nb - 1), h)),
                  kv_spec, kv_spec,
                  pl.BlockSpec((1, nb, width), lambda n, h, s: (n, 0, h))],
        out_specs=pl.BlockSpec((blk, width), lambda n, h, s: (n * nb + jnp.maximum(s - 1, 0), h)),
        out_shape=jax.ShapeDtypeStruct(q2.shape, BF16),
        scratch_shapes=[pltpu.VMEM((hs, seq_len, HEAD_DIM), BF16),
                        pltpu.VMEM((hs, HEAD_DIM, seq_len), BF16),
                        pltpu.VMEM((hs, 2, nb, blk, blk), F32), pltpu.VMEM((hs, 2, 1, blk), F32)],
        compiler_params=_params(3),
        name="moba_prompt",
    )(q2, k2, v2, km)


def _moba_sample_body(pt_ref, q_ref, kn_ref, vn_ref, *rest, n_pages, page_size, n_new, route):
    del pt_ref
    kp = rest[:n_pages]
    vp = rest[n_pages:2 * n_pages]
    rest = rest[2 * n_pages:]
    if route:
        qh_ref, sub_ref, o_ref, i_ref, j_ref, g_ref, s_ref = rest[:7]
        route_scratch = rest[7:]
    else:
        o_ref, s_ref = rest
    rows = n_new * N_HEADS
    page_rows = page_size * N_HEADS
    pages_per_block = MOBA_BLOCK // page_size
    n_past_blocks = n_pages // pages_per_block
    scale = HEAD_DIM ** -0.5

    q = q_ref[...]
    kmeans = []
    for blk in range(n_past_blocks):
        acc = jnp.zeros((N_HEADS, HEAD_DIM), F32)
        for p in range(blk * pages_per_block, (blk + 1) * pages_per_block):
            part = kp[p][...]
            while part.shape[0] > 1:
                half = part.shape[0] // 2
                part = part[:half] + part[half:]
            acc = acc + part[0]
        kmeans.append(acc * (1.0 / MOBA_BLOCK))
    kmean = jnp.concatenate(kmeans, axis=0)
    gate = _nt_dot(q, kmean, precision=HIGHEST)
    r_head = lax.broadcasted_iota(jnp.int32, gate.shape, 0) % N_HEADS
    c_idx = lax.broadcasted_iota(jnp.int32, gate.shape, 1)
    c_blk = c_idx // N_HEADS
    same_head = (c_idx % N_HEADS) == r_head
    rank = jnp.zeros(gate.shape, F32)
    for bp in range(n_past_blocks):
        colval = jnp.max(jnp.where(same_head & (c_blk == bp), gate, NEG_INF), axis=1, keepdims=True)
        beats = (colval > gate) | ((colval == gate) & (bp < c_blk))
        rank = rank + jnp.where(beats, 1.0, 0.0)
    k_sel = min(MOBA_TOPK, n_past_blocks + 1)
    sel = jnp.where(same_head & (rank < k_sel), 1.0, 0.0)

    qb = (q * (scale * LOG2_E)).astype(BF16)
    head_mask = N_HEADS - 1
    p_head = lax.broadcasted_iota(jnp.int32, (rows, page_rows), 0) & head_mask
    p_lane_head = lax.broadcasted_iota(jnp.int32, (rows, page_rows), 1) & head_mask
    page_same_head = p_head == p_lane_head
    for blk in range(n_past_blocks):
        blk_sel = jnp.max(jnp.where(c_blk == blk, sel, 0.0), axis=1, keepdims=True) > 0.5
        keep = page_same_head & blk_sel
        for p in range(blk * pages_per_block, (blk + 1) * pages_per_block):
            kpage = kp[p][...].reshape(page_rows, HEAD_DIM).astype(BF16)
            s = _nt_dot(qb, kpage)
            s_ref[:, p * page_rows:(p + 1) * page_rows] = jnp.where(keep, s, NEG_INF)
    pad = jnp.zeros((LANES - rows, HEAD_DIM), F32)
    kn = jnp.concatenate([kn_ref[...], pad], axis=0).astype(BF16)
    vn = jnp.concatenate([vn_ref[...], pad], axis=0).astype(BF16)
    sn = _nt_dot(qb, kn)
    n_row = lax.broadcasted_iota(jnp.int32, sn.shape, 0)
    n_col = lax.broadcasted_iota(jnp.int32, sn.shape, 1)
    keep_new = ((n_col & head_mask) == (n_row & head_mask)) & (n_col // N_HEADS <= n_row // N_HEADS) \
        & (n_col < rows)
    past = n_pages * page_rows
    s_ref[:, past:past + LANES] = jnp.where(keep_new, sn, NEG_INF)

    m = jnp.max(s_ref[...], axis=1, keepdims=True)
    l = jnp.zeros((rows, 1), F32)
    acc = jnp.zeros((rows, HEAD_DIM), F32)
    for p in range(n_pages):
        pp = jnp.exp2(s_ref[:, p * page_rows:(p + 1) * page_rows] - m)
        l = l + jnp.sum(pp, axis=1, keepdims=True)
        vpage = vp[p][...].reshape(page_rows, HEAD_DIM).astype(BF16)
        acc = acc + jnp.dot(pp.astype(BF16), vpage, preferred_element_type=F32)
    pn = jnp.exp2(s_ref[:, past:past + LANES] - m)
    l = l + jnp.sum(pn, axis=1, keepdims=True)
    acc = acc + jnp.dot(pn.astype(BF16), vn, preferred_element_type=F32)
    o_ref[...] = (acc / l).astype(o_ref.dtype)

    if route:
        head = pl.program_id(0) % PEER_HEADS
        _route_head(head, qh_ref.at[0], qh_ref.at[1], sub_ref.at[0], sub_ref.at[1], route_scratch)

        @pl.when(head == PEER_HEADS - 1)
        def _():
            _route_emit(route_scratch, i_ref, j_ref, g_ref)


def _moba_sample(q8, kn8, vn8, cache_k, cache_v, page_table, layer, *, n_seq, n_new, past_len,
                 route=None):
    page_size = cache_k.shape[2]
    n_pages = page_table.shape[1]
    assert past_len == n_pages * page_size and past_len % MOBA_BLOCK == 0
    assert MOBA_BLOCK % page_size == 0 and n_new <= MOBA_BLOCK and n_new * N_HEADS <= LANES
    assert cache_k.shape[3:] == (N_HEADS, HEAD_DIM) and N_HEADS == SUBLANES
    assert page_size & (page_size - 1) == 0
    rows = n_new * N_HEADS
    row_spec = pl.BlockSpec((rows, HEAD_DIM), lambda s, pt: (s, 0))

    def page_spec(p):
        return pl.BlockSpec((None, None, page_size, N_HEADS, HEAD_DIM),
                            lambda s, pt: (layer, pt[s, p], 0, 0, 0))

    page_specs = [page_spec(p) for p in range(n_pages)]
    in_specs = [row_spec, row_spec, row_spec] + page_specs + page_specs
    args = [q8, kn8, vn8] + [cache_k] * n_pages + [cache_v] * n_pages
    out_specs = [row_spec]
    out_shape = [jax.ShapeDtypeStruct(q8.shape, BF16)]
    scratch = [pltpu.VMEM((rows, n_pages * page_size * N_HEADS + LANES), F32)]
    if route is not None:
        qh3, subkeys = route
        n_tok = qh3.shape[1]
        tm = ROUTE_TILE
        n_slots = PEER_HEADS * PEER_TOPK
        assert n_slots == LANES and n_tok % tm == 0 and (n_tok // tm) * PEER_HEADS == n_seq
        sub3 = subkeys.reshape(2 * PEER_HEADS, PEER_NKEYS, PEER_DKEY).astype(BF16)
        in_specs += [pl.BlockSpec((2, tm, PEER_DKEY), lambda s, pt: (s % PEER_HEADS, s // PEER_HEADS, 0)),
                     pl.BlockSpec((2, PEER_NKEYS, PEER_DKEY), lambda s, pt: (s % PEER_HEADS, 0, 0))]
        args += [qh3, sub3]
        out_specs += [pl.BlockSpec((tm, n_slots), lambda s, pt: (s // PEER_HEADS, 0))] * 3
        out_shape += [jax.ShapeDtypeStruct((n_tok, n_slots), F32)] * 3
        scratch += _route_scratch(tm)
    grid_spec = pltpu.PrefetchScalarGridSpec(
        num_scalar_prefetch=1,
        grid=(n_seq,),
        in_specs=in_specs,
        out_specs=out_specs,
        scratch_shapes=scratch,
    )
    outs = pl.pallas_call(
        functools.partial(_moba_sample_body, n_pages=n_pages, page_size=page_size, n_new=n_new,
                          route=route is not None),
        grid_spec=grid_spec,
        out_shape=out_shape,
        compiler_params=_params(1),
        name="moba_sample_route" if route is not None else "moba_sample",
    )(page_table, *args)
    return outs[0], (tuple(outs[1:]) if route is not None else None)


def _memkv_body(mem_ref, mn_ref, w_ref, mkn_ref, k_ref, v_ref):
    xn = _rms(mem_ref[...], mn_ref[...]).astype(BF16)
    kv = jnp.dot(xn, w_ref[...], preferred_element_type=F32)
    mkn = mkn_ref[...]
    for h in range(MEM_HEADS):
        sl = slice(h * MEM_HEAD_DIM, (h + 1) * MEM_HEAD_DIM)
        k_ref[:, sl] = _rms(kv[:, sl], mkn)
    v_ref[...] = kv[:, D_MEM:]


def _memkv(mem2, mem_norm, w_mem_kv, mk_norm):
    n_rows = mem2.shape[0]
    tm = N_MEM
    spec = pl.BlockSpec((tm, D_MEM), lambda i: (i, 0))
    return pl.pallas_call(
        _memkv_body,
        grid=(n_rows // tm,),
        in_specs=[pl.BlockSpec((tm, D_MODEL), lambda i: (i, 0)), _resident((1, D_MODEL)),
                  _resident((D_MODEL, 2 * D_MEM)), _resident((1, MEM_HEAD_DIM))],
        out_specs=(spec, spec),
        out_shape=(jax.ShapeDtypeStruct((n_rows, D_MEM), F32),) * 2,
        compiler_params=_params(1),
        name="memkv",
    )(mem2, mem_norm, w_mem_kv, mk_norm)


def _memattn_body(mq_ref, k_ref, v_ref, o_ref, *, groups, rows, heads_on_lanes):
    scale = MEM_HEAD_DIM ** -0.5
    for g in range(groups):
        rs = slice(g * rows, (g + 1) * rows)
        for h in range(MEM_HEADS):
            sl = slice(h * MEM_HEAD_DIM, (h + 1) * MEM_HEAD_DIM)
            if heads_on_lanes:
                kh, vh = k_ref[g, :, sl], v_ref[g, :, sl]
            else:
                kh, vh = k_ref[g, :, h, :], v_ref[g, :, h, :]
            qh = mq_ref[rs, sl].astype(BF16)
            s = _nt_dot(qh, kh.astype(BF16)) * scale
            p = jnp.exp(s - jnp.max(s, axis=1, keepdims=True))
            l = jnp.sum(p, axis=1, keepdims=True)
            o = jnp.dot(p.astype(BF16), vh.astype(BF16), preferred_element_type=F32)
            o_ref[rs, sl] = (o / l).astype(o_ref.dtype)


def _memattn(mq2, mem_k, mem_v, mem_spec, *, groups, rows):
    n_tok = mq2.shape[0]
    tm = groups * rows
    row_spec = pl.BlockSpec((tm, D_MEM), lambda i: (i, 0))
    heads_on_lanes = sum(d is not None for d in mem_spec.block_shape) == 3
    return pl.pallas_call(
        functools.partial(_memattn_body, groups=groups, rows=rows, heads_on_lanes=heads_on_lanes),
        grid=(n_tok // tm,),
        in_specs=[row_spec, mem_spec, mem_spec],
        out_specs=row_spec,
        out_shape=jax.ShapeDtypeStruct((n_tok, D_MEM), BF16),
        compiler_params=_params(1),
        name="memattn",
    )(mq2, mem_k, mem_v)


def _post_body(x_ref, cy_ref, o_ref, om_ref, gate_ref, wco_ref, wao_ref, wmo_ref, wo_ref,
               fn_ref, wq_ref, h_ref, xn_ref, qh_ref):
    conv_out = jnp.dot(cy_ref[...], wco_ref[...], preferred_element_type=F32)
    attn_out = jnp.dot(o_ref[...], wao_ref[...], preferred_element_type=F32)
    mem_out = jnp.dot(om_ref[...], wmo_ref[...], preferred_element_type=F32)
    merged = (jax.nn.sigmoid(gate_ref[:, 0:D_MODEL]) * conv_out
              + jax.nn.sigmoid(gate_ref[:, D_MODEL:2 * D_MODEL]) * attn_out
              + jax.nn.sigmoid(gate_ref[:, 2 * D_MODEL:3 * D_MODEL]) * mem_out)
    h = x_ref[...] + jnp.dot(merged.astype(BF16), wo_ref[...], preferred_element_type=F32)
    h_ref[...] = h
    xn = _rms(h, fn_ref[...]).astype(BF16)
    xn_ref[...] = xn
    qh = jnp.dot(xn, wq_ref[...], preferred_element_type=F32)
    for hc in range(2 * PEER_HEADS):
        qh_ref[hc] = qh[:, hc * PEER_DKEY:(hc + 1) * PEER_DKEY].astype(BF16)


def _post(x2, cy, o2, om, gates, wco, wao, wmo, wo, ffn_norm, wq):
    n_tok = x2.shape[0]
    tm = min(TOKEN_TILE, n_tok)

    def rows(width):
        return pl.BlockSpec((tm, width), lambda i: (i, 0))

    return pl.pallas_call(
        _post_body,
        grid=(n_tok // tm,),
        in_specs=[rows(D_MODEL), rows(D_CONV), rows(D_ATTN), rows(D_MEM), rows(3 * D_MODEL),
                  _resident(wco.shape), _resident(wao.shape), _resident(wmo.shape),
                  _resident(wo.shape), _resident((1, D_MODEL)), _resident(wq.shape)],
        out_specs=(rows(D_MODEL), rows(D_MODEL),
                   pl.BlockSpec((2 * PEER_HEADS, tm, PEER_DKEY), lambda i: (0, i, 0))),
        out_shape=(jax.ShapeDtypeStruct((n_tok, D_MODEL), F32),
                   jax.ShapeDtypeStruct((n_tok, D_MODEL), BF16),
                   jax.ShapeDtypeStruct((2 * PEER_HEADS, n_tok, PEER_DKEY), BF16)),
        compiler_params=_params(1),
        name="post",
    )(x2, cy, o2, om, gates, wco, wao, wmo, wo, ffn_norm, wq)


_CAND_PAIRS = tuple((a, b) for a in range(PEER_TOPK) for b in range(PEER_TOPK)
                    if (a + 1) * (b + 1) <= PEER_TOPK)


def _tree_argmax(val, idx):
    chunk = 2 * SUBLANES
    if val.shape[0] > chunk:
        parts = [_tree_argmax(val[c:c + chunk], idx[c:c + chunk])
                 for c in range(0, val.shape[0], chunk)]
        val = jnp.stack([p[0] for p in parts])
        idx = jnp.stack([p[1] for p in parts])
    while val.shape[0] > 1:
        n = val.shape[0]
        pairs = n // 2
        v2 = val[:2 * pairs].reshape((pairs, 2) + val.shape[1:])
        i2 = idx[:2 * pairs].reshape((pairs, 2) + idx.shape[1:])
        left = v2[:, 0] >= v2[:, 1]
        new_val = jnp.maximum(v2[:, 0], v2[:, 1])
        new_idx = jnp.where(left, i2[:, 0], i2[:, 1])
        if n % 2:
            new_val = jnp.concatenate([new_val, val[2 * pairs:]], axis=0)
            new_idx = jnp.concatenate([new_idx, idx[2 * pairs:]], axis=0)
        val, idx = new_val, new_idx
    return val[0], idx[0]


def _route_scratch(tm):
    tile = (tm // LANES, LANES)
    keys = pltpu.VMEM((PEER_NKEYS,) + tile, F32)
    small = pltpu.VMEM((PEER_TOPK,) + tile, F32)
    slots = pltpu.VMEM((PEER_HEADS * PEER_TOPK,) + tile, F32)
    return [keys, keys, small, small, small, small, small, small, slots, slots, slots]


def _route_head(h, qh1_ref, qh2_ref, sub1_ref, sub2_ref, scratch):
    (x1_ref, x2_ref, st1_ref, ix1_ref, st2_ref, ix2_ref, best_ref, flat_ref,
     si_ref, sj_ref, sg_ref) = scratch
    n_sub = x1_ref.shape[1]
    k = PEER_TOPK
    tile = (n_sub, LANES)
    key_id = lax.broadcasted_iota(jnp.int32, (PEER_NKEYS,) + tile, 0).astype(F32)
    flat_id = jnp.stack([jnp.full(tile, float(a * k + b), F32) for a, b in _CAND_PAIRS])

    def scores_into(sub_ref, qh_ref, x_ref):
        for s in range(n_sub):
            x_ref[:, s, :] = _nt_dot(sub_ref[...], qh_ref[s * LANES:(s + 1) * LANES, :])

    def pop_max(x_ref, r, st_ref, ix_ref):
        x = x_ref[...]
        m, f = _tree_argmax(x, key_id)
        st_ref[r] = m
        ix_ref[r] = f
        x_ref[...] = jnp.where(key_id == f[None], NEG_INF, x)

    def lookup(rank, table):
        out = jnp.zeros(rank.shape, F32)
        for a in range(k):
            out = jnp.where(rank == float(a), table[a][None], out)
        return out

    scores_into(sub1_ref, qh1_ref, x1_ref)
    scores_into(sub2_ref, qh2_ref, x2_ref)

    def rnd1(r, c):
        pop_max(x1_ref, r, st1_ref, ix1_ref)
        pop_max(x2_ref, r, st2_ref, ix2_ref)
        return c

    lax.fori_loop(0, k, rnd1, 0)
    st1 = st1_ref[...]
    st2 = st2_ref[...]
    cand = jnp.stack([st1[a] + st2[b] for a, b in _CAND_PAIRS])

    def rnd2(r, x):
        m, f = _tree_argmax(x, flat_id)
        best_ref[r] = m
        flat_ref[r] = f
        return jnp.where(flat_id == f[None], NEG_INF, x)

    lax.fori_loop(0, k, rnd2, cand)
    best = best_ref[...]
    flat = flat_ref[...]
    rank_a = jnp.floor(flat * (1.0 / k))
    rank_b = flat - rank_a * k
    e = jnp.exp(best - best[0:1])
    rows = pl.ds(pl.multiple_of(h * k, k), k)
    si_ref[rows] = lookup(rank_a, ix1_ref[...])
    sj_ref[rows] = lookup(rank_b, ix2_ref[...])
    sg_ref[rows] = e / jnp.sum(e, axis=0, keepdims=True)


def _route_emit(scratch, i_ref, j_ref, g_ref):
    si_ref, sj_ref, sg_ref = scratch[-3:]
    for s in range(si_ref.shape[1]):
        rows = slice(s * LANES, (s + 1) * LANES)
        i_ref[rows, :] = si_ref[:, s, :].T
        j_ref[rows, :] = sj_ref[:, s, :].T
        g_ref[rows, :] = sg_ref[:, s, :].T


def _route_body(qh_ref, sub_ref, i_ref, j_ref, g_ref, *scratch):
    def head(h, carry):
        _route_head(h, qh_ref.at[2 * h], qh_ref.at[2 * h + 1], sub_ref.at[2 * h],
                    sub_ref.at[2 * h + 1], scratch)
        return carry

    lax.fori_loop(0, PEER_HEADS, head, 0)
    _route_emit(scratch, i_ref, j_ref, g_ref)


def _route(qh3, subkeys):
    n_tok = qh3.shape[1]
    tm = min(ROUTE_TILE, n_tok)
    n_slots = PEER_HEADS * PEER_TOPK
    assert n_slots == LANES and tm % LANES == 0 and n_tok % tm == 0
    sub3 = subkeys.reshape(2 * PEER_HEADS, PEER_NKEYS, PEER_DKEY).astype(BF16)
    slot_spec = pl.BlockSpec((tm, n_slots), lambda i: (i, 0))
    return pl.pallas_call(
        _route_body,
        grid=(n_tok // tm,),
        in_specs=[pl.BlockSpec((2 * PEER_HEADS, tm, PEER_DKEY), lambda i: (0, i, 0)),
                  _resident(sub3.shape)],
        out_specs=(slot_spec,) * 3,
        out_shape=(jax.ShapeDtypeStruct((n_tok, n_slots), F32),) * 3,
        scratch_shapes=_route_scratch(tm),
        compiler_params=_params(1),
        name="peer_route",
    )(qh3, sub3)


WBUILD_GROUP = 16
WBUILD_PITCH = PEER_NKEYS + SUBLANES


def _wbuild_body(i_ref, j_ref, g_ref, w_ref, stage_a_ref, stage_b_ref):
    tm = i_ref.shape[0]
    key = lax.broadcasted_iota(jnp.int32, (PEER_NKEYS, LANES), 0).astype(F32)

    def build(gi, stage_ref):
        base = pl.multiple_of(gi * WBUILD_GROUP, WBUILD_GROUP)
        for tt in range(WBUILD_GROUP):
            irow = i_ref[pl.ds(base + tt, 1), :]
            jrow = j_ref[pl.ds(base + tt, 1), :]
            grow = g_ref[pl.ds(base + tt, 1), :]
            a = jnp.where(key == irow, grow, 0.0).astype(BF16)
            b = jnp.where(key == jrow, 1.0, 0.0).astype(BF16)
            stage_ref[tt * WBUILD_PITCH:tt * WBUILD_PITCH + PEER_NKEYS, :] = _nt_dot(a, b)

    def drain(gi, stage_ref):
        base = pl.multiple_of(gi * WBUILD_GROUP, WBUILD_GROUP)
        for i in range(PEER_NKEYS):
            tile = stage_ref[pl.ds(i, WBUILD_GROUP, stride=WBUILD_PITCH), :]
            w_ref[pl.ds(base, WBUILD_GROUP), i * PEER_NKEYS:(i + 1) * PEER_NKEYS] = tile.astype(BF16)

    n_groups = tm // WBUILD_GROUP
    assert n_groups % 2 == 0
    build(0, stage_a_ref)

    def step(p, carry):
        build(2 * p + 1, stage_b_ref)
        drain(2 * p, stage_a_ref)
        build(2 * p + 2, stage_a_ref)
        drain(2 * p + 1, stage_b_ref)
        return carry

    lax.fori_loop(0, n_groups // 2 - 1, step, 0)
    build(n_groups - 1, stage_b_ref)
    drain(n_groups - 2, stage_a_ref)
    drain(n_groups - 1, stage_b_ref)


def _wbuild(si, sj, sg):
    n_tok = si.shape[0]
    tm = min(WBUILD_TILE, n_tok)
    assert tm % WBUILD_GROUP == 0
    slot_spec = pl.BlockSpec((tm, LANES), lambda i: (i, 0))
    return pl.pallas_call(
        _wbuild_body,
        grid=(n_tok // tm,),
        in_specs=[slot_spec] * 3,
        out_specs=pl.BlockSpec((tm, PEER_N_EXPERTS), lambda i: (i, 0)),
        out_shape=jax.ShapeDtypeStruct((n_tok, PEER_N_EXPERTS), BF16),
        scratch_shapes=[pltpu.VMEM((WBUILD_GROUP * WBUILD_PITCH, PEER_NKEYS), F32)] * 2,
        compiler_params=_params(1),
        name="peer_wbuild",
    )(si, sj, sg)


def _peer_body(xn_ref, u_ref, v_ref, w_ref, h_ref, y_ref):
    @pl.when(pl.program_id(1) == 0)
    def _():
        y_ref[...] = h_ref[...]

    xn = xn_ref[...]
    sub = PEER_EXPERT_SUB
    for s in range(u_ref.shape[0] // sub):
        sl = slice(s * sub, (s + 1) * sub)
        z = _nt_dot(xn, u_ref[sl, :])
        a = 0.5 * z * (1.0 + lax.erf(z * (2.0 ** -0.5)))
        a = a * w_ref[:, sl].astype(F32)
        y_ref[...] += jnp.dot(a.astype(BF16), v_ref[sl, :], preferred_element_type=F32)


def _peer_dense(xn, w2, h, peer_u, peer_v):
    n_tok = xn.shape[0]
    tm = min(PEER_TOKEN_TILE, n_tok)
    ec = PEER_EXPERT_CHUNK
    assert n_tok % tm == 0 and PEER_N_EXPERTS % ec == 0 and ec % PEER_EXPERT_SUB == 0
    tok_spec = pl.BlockSpec((tm, D_MODEL), lambda i, c: (i, 0))
    exp_spec = pl.BlockSpec((ec, D_MODEL), lambda i, c: (c, 0))
    return pl.pallas_call(
        _peer_body,
        grid=(n_tok // tm, PEER_N_EXPERTS // ec),
        in_specs=[tok_spec, exp_spec, exp_spec, pl.BlockSpec((tm, ec), lambda i, c: (i, c)),
                  pl.BlockSpec((tm, D_MODEL), lambda i, c: (i, 0), pipeline_mode=pl.Buffered(1))],
        out_specs=tok_spec,
        out_shape=jax.ShapeDtypeStruct((n_tok, D_MODEL), F32),
        compiler_params=_params(2),
        name="peer_dense",
    )(xn, peer_u, peer_v, w2, h)


def _row(v):
    return v.reshape(1, -1)


def _peer(h, xn, qh3, lw, slots=None):
    if slots is None:
        slots = _route(qh3, lw["subkeys"])
    w2 = _wbuild(*slots)
    return _peer_dense(xn, w2, h, lw["peer_u"], lw["peer_v"])


def kernel(x_prompt, x_sample, mem_prompt, cache_k, cache_v, cache_conv, cache_mem_k, cache_mem_v,
           page_table, attn_norm, w_in, conv_w, w_conv_out, q_norm, k_norm, w_attn_out,
           mem_norm, w_mem_kv, mk_norm, mq_norm, w_mem_out, w_o,
           ffn_norm, peer_wq, peer_subkeys, peer_u, peer_v):
    n_p, t_p, _ = x_prompt.shape
    n_s, t_s, _ = x_sample.shape
    depth = attn_norm.shape[0]
    past_len = page_table.shape[1] * cache_k.shape[2]
    h_p = x_prompt.reshape(n_p * t_p, D_MODEL)
    h_s = x_sample.reshape(n_s * t_s, D_MODEL)
    outs = [[] for _ in range(8)]
    for l in range(depth):
        lw = dict(
            wco=w_conv_out[l].astype(BF16), wao=w_attn_out[l].astype(BF16),
            wmo=w_mem_out[l].astype(BF16), wo=w_o[l].astype(BF16), wq=peer_wq[l].astype(BF16),
            ffn_norm=_row(ffn_norm[l]), subkeys=peer_subkeys[l],
            peer_u=peer_u[l].astype(BF16), peer_v=peer_v[l].astype(BF16))
        in_w = (_row(attn_norm[l]), w_in[l].astype(BF16), _row(q_norm[l]), _row(k_norm[l]),
                _row(mq_norm[l]), conv_w[l])

        mem_k, mem_v = _memkv(mem_prompt.reshape(n_p * N_MEM, D_MODEL), _row(mem_norm[l]),
                              w_mem_kv[l].astype(BF16), _row(mk_norm[l]))
        zero_state = jnp.zeros((SUBLANES, D_CONV), F32)
        cy, u, q, k, v, mq, gates, kmean = _inproj(h_p, *in_w, zero_state, zero_state, seq_len=t_p)
        o = _moba_prompt(q, k, v, kmean, n_seq=n_p, seq_len=t_p)
        steps_per_mem = t_p // TOKEN_TILE

        prompt_mem = pl.BlockSpec((1, N_MEM, D_MEM), lambda i: (i // steps_per_mem, 0, 0))
        om = _memattn(mq, mem_k.reshape(n_p, N_MEM, D_MEM), mem_v.reshape(n_p, N_MEM, D_MEM),
                      prompt_mem, groups=1, rows=TOKEN_TILE)
        h_mid_p, xn_p, qh3_p = _post(h_p, cy, o, om, gates, lw["wco"], lw["wao"], lw["wmo"], lw["wo"],
                                     lw["ffn_norm"], lw["wq"])
        outs[0].append(k.reshape(n_p, t_p, N_HEADS, HEAD_DIM))
        outs[1].append(v.reshape(n_p, t_p, N_HEADS, HEAD_DIM))
        outs[2].append(u.reshape(n_p, t_p, D_CONV)[:, t_p - (CONV_W - 1):])
        outs[3].append(mem_k.reshape(n_p, N_MEM, MEM_HEADS, MEM_HEAD_DIM))
        outs[4].append(mem_v.reshape(n_p, N_MEM, MEM_HEADS, MEM_HEAD_DIM))

        state = cache_conv[l]
        s0x = jnp.repeat(state[:, 0], t_s, axis=0)
        s1x = jnp.repeat(state[:, 1], t_s, axis=0)
        cy, u, q, k, v, mq, gates, _ = _inproj(h_s, *in_w, s0x, s1x, seq_len=t_s)
        k4 = k.reshape(n_s, t_s, N_HEADS, HEAD_DIM)
        v4 = v.reshape(n_s, t_s, N_HEADS, HEAD_DIM)
        rows = n_s * t_s * N_HEADS
        ride = (n_p * t_p) % ROUTE_TILE == 0 and (n_p * t_p // ROUTE_TILE) * PEER_HEADS == n_s
        o, slots_p = _moba_sample(q.reshape(rows, HEAD_DIM), k4.reshape(rows, HEAD_DIM),
                                  v4.reshape(rows, HEAD_DIM), cache_k, cache_v, page_table, l,
                                  n_seq=n_s, n_new=t_s, past_len=past_len,
                                  route=(qh3_p, lw["subkeys"]) if ride else None)
        h_p = _peer(h_mid_p, xn_p, qh3_p, lw, slots=slots_p)
        o = o.reshape(n_s * t_s, D_ATTN)
        groups = SUBLANES

        sample_mem = pl.BlockSpec((None, groups, N_MEM, MEM_HEADS, MEM_HEAD_DIM),
                                  lambda i: (l, i, 0, 0, 0))
        om = _memattn(mq, cache_mem_k, cache_mem_v, sample_mem, groups=groups, rows=t_s)
        h_mid_s, xn_s, qh3_s = _post(h_s, cy, o, om, gates, lw["wco"], lw["wao"], lw["wmo"], lw["wo"],
                                     lw["ffn_norm"], lw["wq"])
        h_s = _peer(h_mid_s, xn_s, qh3_s, lw)
        outs[5].append(k4)
        outs[6].append(v4)
        ext = jnp.concatenate([state, u.reshape(n_s, t_s, D_CONV)], axis=1)
        outs[7].append(ext[:, t_s:])
    stacked = [jnp.stack(o) for o in outs]
    return (h_p.reshape(n_p, t_p, D_MODEL), h_s.reshape(n_s, t_s, D_MODEL), *stacked)
```

```python
import functools

import jax
import jax.numpy as jnp
from jax import lax
from jax.experimental import pallas as pl
from jax.experimental.pallas import tpu as pltpu

F32 = jnp.float32
BF16 = jnp.bfloat16
HIGHEST = lax.Precision.HIGHEST
NEG_INF = float("-inf")
LOG2_E = 1.4426950408889634

SUBLANES = 8
LANES = 128
VMEM_LIMIT_BYTES = 56 * 1024 * 1024

EPS = 1e-6
D_MODEL = 1024
D_CONV = 512
CONV_W = 3
N_HEADS = 8
HEAD_DIM = 128
D_ATTN = N_HEADS * HEAD_DIM
MOBA_BLOCK = 256
MOBA_TOPK = 3
N_MEM = 256
MEM_HEADS = 4
MEM_HEAD_DIM = 128
D_MEM = MEM_HEADS * MEM_HEAD_DIM
PEER_HEADS = 8
PEER_NKEYS = 128
PEER_DKEY = 128
PEER_TOPK = 16
PEER_N_EXPERTS = PEER_NKEYS * PEER_NKEYS
D_PEER_Q = PEER_HEADS * 2 * PEER_DKEY
D_IN = 3 * D_CONV + 3 * D_ATTN + D_MEM + 3 * D_MODEL
OFF_CB, OFF_CC, OFF_CH = 0, D_CONV, 2 * D_CONV
OFF_Q = 3 * D_CONV
OFF_K = OFF_Q + D_ATTN
OFF_V = OFF_K + D_ATTN
OFF_MQ = OFF_V + D_ATTN
OFF_G = OFF_MQ + D_MEM

TOKEN_TILE = 256
MOBA_HEADS_PER_STEP = 2
ROUTE_TILE = 1024
WBUILD_TILE = 512
PEER_TOKEN_TILE = 1024
PEER_EXPERT_CHUNK = 2048
PEER_EXPERT_SUB = 512


def _nt_dot(a, b, precision=None):
    return lax.dot_general(a, b, (((1,), (1,)), ((), ())), precision=precision,
                           preferred_element_type=F32)


def _params(n_grid_dims):
    return pltpu.CompilerParams(dimension_semantics=("arbitrary",) * n_grid_dims,
                                vmem_limit_bytes=VMEM_LIMIT_BYTES)


def _rms(x, gain):
    r = lax.rsqrt(jnp.mean(x * x, axis=-1, keepdims=True) + EPS)
    return x * r * gain


def _resident(shape):
    nd = len(shape)
    return pl.BlockSpec(shape, lambda *_: (0,) * nd, pipeline_mode=pl.Buffered(1))


def _inproj_body(x_ref, an_ref, w_ref, qn_ref, kn_ref, mqn_ref, cw_ref, s0_ref, s1_ref,
                 cy_ref, u_ref, q_ref, k_ref, v_ref, mq_ref, gate_ref, kmean_ref,
                 ubuf_ref, *, tiles_per_seq, seq_len):
    tm = x_ref.shape[0]
    xn = _rms(x_ref[...], an_ref[...]).astype(BF16)

    def proj(off, width):
        return jnp.dot(xn, w_ref[:, off:off + width], preferred_element_type=F32)

    cb = proj(OFF_CB, D_CONV)
    u = proj(OFF_CC, D_CONV) * proj(OFF_CH, D_CONV)
    u_ref[...] = u
    if seq_len >= tm:
        @pl.when(pl.program_id(0) % tiles_per_seq == 0)
        def _():
            ubuf_ref[0:SUBLANES, :] = jnp.zeros((SUBLANES, D_CONV), F32)
    else:
        ubuf_ref[0:SUBLANES, :] = jnp.zeros((SUBLANES, D_CONV), F32)
    ubuf_ref[SUBLANES:SUBLANES + tm, :] = u
    um1 = ubuf_ref[SUBLANES - 1:SUBLANES - 1 + tm, :]
    um2 = ubuf_ref[SUBLANES - 2:SUBLANES - 2 + tm, :]
    if seq_len >= tm:
        ubuf_ref[0:SUBLANES, :] = ubuf_ref[tm:tm + SUBLANES, :]
    else:
        assert seq_len == SUBLANES
        tl = lax.broadcasted_iota(jnp.int32, (tm, D_CONV), 0) % seq_len
        s0 = s0_ref[...]
        s1 = s1_ref[...]
        um1 = jnp.where(tl == 0, s1, um1)
        um2 = jnp.where(tl == 0, s0, jnp.where(tl == 1, s1, um2))
    cw = cw_ref[...]
    y = cw[0:1, :] * um2 + cw[1:2, :] * um1 + cw[2:3, :] * u
    cy_ref[...] = (cb * y).astype(BF16)

    qf = proj(OFF_Q, D_ATTN)
    kf = proj(OFF_K, D_ATTN)
    qn = qn_ref[...]
    kn = kn_ref[...]
    for h in range(N_HEADS):
        sl = slice(h * HEAD_DIM, (h + 1) * HEAD_DIM)
        q_ref[:, sl] = _rms(qf[:, sl], qn)
        k_ref[:, sl] = _rms(kf[:, sl], kn)
    kmean_ref[0] = jnp.mean(k_ref[...], axis=0, keepdims=True)
    v_ref[...] = proj(OFF_V, D_ATTN)
    mqf = proj(OFF_MQ, D_MEM)
    mqn = mqn_ref[...]
    for h in range(MEM_HEADS):
        sl = slice(h * MEM_HEAD_DIM, (h + 1) * MEM_HEAD_DIM)
        mq_ref[:, sl] = _rms(mqf[:, sl], mqn)
    gate_ref[...] = proj(OFF_G, 3 * D_MODEL)


def _inproj(x2, attn_norm, w_in, q_norm, k_norm, mq_norm, conv_w, s0x, s1x, *, seq_len):
    n_tok = x2.shape[0]
    tm = min(TOKEN_TILE, n_tok)
    assert n_tok % tm == 0 and (seq_len % tm == 0 or seq_len == SUBLANES)
    n_tiles = n_tok // tm
    tiles_per_seq = max(seq_len // tm, 1)
    long_seq = seq_len >= tm

    def rows(width):
        return pl.BlockSpec((tm, width), lambda i: (i, 0))

    state_spec = _resident(s0x.shape) if long_seq else rows(D_CONV)
    out_shape = (
        jax.ShapeDtypeStruct((n_tok, D_CONV), BF16),
        jax.ShapeDtypeStruct((n_tok, D_CONV), F32),
        jax.ShapeDtypeStruct((n_tok, D_ATTN), F32),
        jax.ShapeDtypeStruct((n_tok, D_ATTN), F32),
        jax.ShapeDtypeStruct((n_tok, D_ATTN), F32),
        jax.ShapeDtypeStruct((n_tok, D_MEM), F32),
        jax.ShapeDtypeStruct((n_tok, 3 * D_MODEL), F32),
        jax.ShapeDtypeStruct((n_tiles, 1, D_ATTN), F32),
    )
    return pl.pallas_call(
        functools.partial(_inproj_body, tiles_per_seq=tiles_per_seq, seq_len=seq_len),
        grid=(n_tiles,),
        in_specs=[rows(D_MODEL), _resident((1, D_MODEL)), _resident((D_MODEL, D_IN)),
                  _resident((1, HEAD_DIM)), _resident((1, HEAD_DIM)), _resident((1, MEM_HEAD_DIM)),
                  _resident((CONV_W, D_CONV)), state_spec, state_spec],
        out_specs=(rows(D_CONV), rows(D_CONV), rows(D_ATTN), rows(D_ATTN), rows(D_ATTN),
                   rows(D_MEM), rows(3 * D_MODEL),
                   pl.BlockSpec((1, 1, D_ATTN), lambda i: (i, 0, 0))),
        out_shape=out_shape,
        scratch_shapes=[pltpu.VMEM((tm + 2 * SUBLANES, D_CONV), F32)],
        compiler_params=_params(1),
        name="inproj",
    )(x2, attn_norm, w_in, q_norm, k_norm, mq_norm, conv_w, s0x, s1x)


def _moba_prompt_body(q_ref, k_ref, v_ref, km_ref, o_ref, kb_ref, vt_ref, sc_ref, m_ref,
                      *, n_blocks):
    s_id = pl.program_id(2)
    blk = MOBA_BLOCK
    scale = HEAD_DIM ** -0.5
    heads = kb_ref.shape[0]

    @pl.when(s_id == 0)
    def _():
        for hh in range(heads):
            lanes = slice(hh * HEAD_DIM, (hh + 1) * HEAD_DIM)
            kb_ref[hh] = k_ref[:, lanes].astype(BF16)
            for j in range(n_blocks):
                vt_ref[hh, :, j * blk:(j + 1) * blk] = v_ref[j * blk:(j + 1) * blk, lanes].T.astype(BF16)

    key_row = lax.broadcasted_iota(jnp.int32, (blk, blk), 0)
    query_col = lax.broadcasted_iota(jnp.int32, (blk, blk), 1)

    def score_block(own, hh):
        lanes = slice(hh * HEAD_DIM, (hh + 1) * HEAD_DIM)
        q = q_ref[:, lanes]
        gate = _nt_dot(km_ref[0, :, lanes], q, precision=HIGHEST)
        blk_id = lax.broadcasted_iota(jnp.int32, gate.shape, 0)
        rank = jnp.zeros(gate.shape, F32)
        for jp in range(own):
            other = gate[jp:jp + 1, :]
            beats = (other > gate) | ((other == gate) & (jp < blk_id))
            rank = rank + jnp.where(beats, 1.0, 0.0)
        k_sel = min(MOBA_TOPK, n_blocks)
        sel = jnp.where((rank < k_sel) & (blk_id < own), 1.0, 0.0)
        qt = (q * (scale * LOG2_E)).T.astype(BF16)
        buf = own % 2
        m = None
        for j in range(own + 1):
            s = jnp.dot(kb_ref[hh, j * blk:(j + 1) * blk, :], qt, preferred_element_type=F32)
            if j == own:
                s = jnp.where(key_row <= query_col, s, NEG_INF)
            else:
                s = jnp.where(sel[j:j + 1, :] > 0.5, s, NEG_INF)
            sc_ref[hh, buf, j] = s
            mj = jnp.max(s, axis=0, keepdims=True)
            m = mj if m is None else jnp.maximum(m, mj)
        m_ref[hh, buf] = m

    def finish_block(own, hh):
        buf = own % 2
        m = m_ref[hh, buf]
        l = jnp.zeros((1, blk), F32)
        acc = jnp.zeros((HEAD_DIM, blk), F32)
        for j in range(own + 1):
            p = jnp.exp2(sc_ref[hh, buf, j] - m)
            l = l + jnp.sum(p, axis=0, keepdims=True)
            acc = acc + jnp.dot(vt_ref[hh, :, j * blk:(j + 1) * blk], p.astype(BF16),
                                preferred_element_type=F32)
        o_ref[:, hh * HEAD_DIM:(hh + 1) * HEAD_DIM] = (acc / l).T.astype(o_ref.dtype)

    for step in range(n_blocks + 1):
        @pl.when(s_id == step)
        def _(step=step):
            for hh in range(heads):
                if step < n_blocks:
                    score_block(step, hh)
                if step >= 1:
                    finish_block(step - 1, hh)


def _moba_prompt(q2, k2, v2, kmean, *, n_seq, seq_len):
    blk = MOBA_BLOCK
    assert seq_len % blk == 0 and TOKEN_TILE == blk
    nb = seq_len // blk
    km = kmean.reshape(n_seq, nb, D_ATTN)
    hs = MOBA_HEADS_PER_STEP
    width = hs * HEAD_DIM
    assert N_HEADS % hs == 0
    kv_spec = pl.BlockSpec((seq_len, width), lambda n, h, s: (n, h))
    return pl.pallas_call(
        functools.partial(_moba_prompt_body, n_blocks=nb),
        grid=(n_seq, N_HEADS // hs, nb + 1),
        in_specs=[pl.BlockSpec((blk, width), lambda n, h, s: (n * nb + jnp.minimum(s, nb - 1), h)),
                  kv_spec, kv_spec,
                  pl.BlockSpec((1, nb, width), lambda n, h, s: (n, 0, h))],
        out_specs=pl.BlockSpec((blk, width), lambda n, h, s: (n * nb + jnp.maximum(s - 1, 0), h)),
        out_shape=jax.ShapeDtypeStruct(q2.shape, BF16),
        scratch_shapes=[pltpu.VMEM((hs, seq_len, HEAD_DIM), BF16),
                        pltpu.VMEM((hs, HEAD_DIM, seq_len), BF16),
                        pltpu.VMEM((hs, 2, nb, blk, blk), F32), pltpu.VMEM((hs, 2, 1, blk), F32)],
        compiler_params=_params(3),
        name="moba_prompt",
    )(q2, k2, v2, km)


def _moba_sample_body(pt_ref, q_ref, kn_ref, vn_ref, *rest, n_pages, page_size, n_new, route):
    del pt_ref
    kp = rest[:n_pages]
    vp = rest[n_pages:2 * n_pages]
    rest = rest[2 * n_pages:]
    if route:
        qh_ref, sub_ref, o_ref, i_ref, j_ref, g_ref, s_ref = rest[:7]
        route_scratch = rest[7:]
    else:
        o_ref, s_ref = rest
    rows = n_new * N_HEADS
    page_rows = page_size * N_HEADS
    pages_per_block = MOBA_BLOCK // page_size
    n_past_blocks = n_pages // pages_per_block
    scale = HEAD_DIM ** -0.5

    q = q_ref[...]
    qb = (q * (scale * LOG2_E)).astype(BF16)
    kmeans = []
    for blk in range(n_past_blocks):
        acc = jnp.zeros((N_HEADS, HEAD_DIM), F32)
        for p in range(blk * pages_per_block, (blk + 1) * pages_per_block):
            part = kp[p][...]
            s_ref[:, p * page_rows:(p + 1) * page_rows] = _nt_dot(
                qb, part.reshape(page_rows, HEAD_DIM).astype(BF16))
            while part.shape[0] > 1:
                half = part.shape[0] // 2
                part = part[:half] + part[half:]
            acc = acc + part[0]
        kmeans.append(acc * (1.0 / MOBA_BLOCK))
    kmean = jnp.concatenate(kmeans, axis=0)
    gate = _nt_dot(q, kmean, precision=HIGHEST)
    r_head = lax.broadcasted_iota(jnp.int32, gate.shape, 0) % N_HEADS
    c_idx = lax.broadcasted_iota(jnp.int32, gate.shape, 1)
    c_blk = c_idx // N_HEADS
    same_head = (c_idx % N_HEADS) == r_head
    rank = jnp.zeros(gate.shape, F32)
    for bp in range(n_past_blocks):
        colval = jnp.max(jnp.where(same_head & (c_blk == bp), gate, NEG_INF), axis=1, keepdims=True)
        beats = (colval > gate) | ((colval == gate) & (bp < c_blk))
        rank = rank + jnp.where(beats, 1.0, 0.0)
    k_sel = min(MOBA_TOPK, n_past_blocks + 1)
    sel = jnp.where(same_head & (rank < k_sel), 1.0, 0.0)

    head_mask = N_HEADS - 1
    n_row = lax.broadcasted_iota(jnp.int32, (rows, LANES), 0)
    n_col = lax.broadcasted_iota(jnp.int32, (rows, LANES), 1)
    same_head_t = (n_col & head_mask) == (n_row & head_mask)
    m_t = jnp.full((rows, LANES), NEG_INF, F32)
    for blk in range(n_past_blocks):
        blk_sel = jnp.max(jnp.where(c_blk == blk, sel, 0.0), axis=1, keepdims=True) > 0.5
        bias_t = jnp.where(same_head_t & blk_sel, 0.0, NEG_INF)
        for t in range(blk * pages_per_block * page_rows // LANES,
                       (blk + 1) * pages_per_block * page_rows // LANES):
            s = s_ref[:, t * LANES:(t + 1) * LANES] + bias_t
            s_ref[:, t * LANES:(t + 1) * LANES] = s
            m_t = jnp.maximum(m_t, s)
    pad = jnp.zeros((LANES - rows, HEAD_DIM), F32)
    kn = jnp.concatenate([kn_ref[...], pad], axis=0).astype(BF16)
    vn = jnp.concatenate([vn_ref[...], pad], axis=0).astype(BF16)
    keep_new = same_head_t & (n_col // N_HEADS <= n_row // N_HEADS) & (n_col < rows)
    sn = jnp.where(keep_new, _nt_dot(qb, kn), NEG_INF)
    past = n_pages * page_rows
    s_ref[:, past:past + LANES] = sn
    m = jnp.max(jnp.maximum(m_t, sn), axis=1, keepdims=True)
    l = jnp.zeros((rows, 1), F32)
    acc = jnp.zeros((rows, HEAD_DIM), F32)
    for p in range(n_pages):
        pp = jnp.exp2(s_ref[:, p * page_rows:(p + 1) * page_rows] - m)
        l = l + jnp.sum(pp, axis=1, keepdims=True)
        vpage = vp[p][...].reshape(page_rows, HEAD_DIM).astype(BF16)
        acc = acc + jnp.dot(pp.astype(BF16), vpage, preferred_element_type=F32)
    pn = jnp.exp2(s_ref[:, past:past + LANES] - m)
    l = l + jnp.sum(pn, axis=1, keepdims=True)
    acc = acc + jnp.dot(pn.astype(BF16), vn, preferred_element_type=F32)
    o_ref[...] = (acc / l).astype(o_ref.dtype)

    if route:
        head = pl.program_id(0) % PEER_HEADS
        _route_head(head, qh_ref.at[0], qh_ref.at[1], sub_ref.at[0], sub_ref.at[1], route_scratch)

        @pl.when(head == PEER_HEADS - 1)
        def _():
            _route_emit(route_scratch, i_ref, j_ref, g_ref)


def _moba_sample(q8, kn8, vn8, cache_k, cache_v, page_table, layer, *, n_seq, n_new, past_len,
                 route=None):
    page_size = cache_k.shape[2]
    n_pages = page_table.shape[1]
    assert past_len == n_pages * page_size and past_len % MOBA_BLOCK == 0
    assert MOBA_BLOCK % page_size == 0 and n_new <= MOBA_BLOCK and n_new * N_HEADS <= LANES
    assert cache_k.shape[3:] == (N_HEADS, HEAD_DIM) and N_HEADS == SUBLANES
    assert page_size & (page_size - 1) == 0
    rows = n_new * N_HEADS
    row_spec = pl.BlockSpec((rows, HEAD_DIM), lambda s, pt: (s, 0))

    def page_spec(p):
        return pl.BlockSpec((None, None, page_size, N_HEADS, HEAD_DIM),
                            lambda s, pt: (layer, pt[s, p], 0, 0, 0))

    page_specs = [page_spec(p) for p in range(n_pages)]
    in_specs = [row_spec, row_spec, row_spec] + page_specs + page_specs
    args = [q8, kn8, vn8] + [cache_k] * n_pages + [cache_v] * n_pages
    out_specs = [row_spec]
    out_shape = [jax.ShapeDtypeStruct(q8.shape, BF16)]
    scratch = [pltpu.VMEM((rows, n_pages * page_size * N_HEADS + LANES), F32)]
    if route is not None:
        qh3, subkeys = route
        n_tok = qh3.shape[1]
        tm = ROUTE_TILE
        n_slots = PEER_HEADS * PEER_TOPK
        assert n_slots == LANES and n_tok % tm == 0 and (n_tok // tm) * PEER_HEADS == n_seq
        sub3 = subkeys.reshape(2 * PEER_HEADS, PEER_NKEYS, PEER_DKEY).astype(BF16)
        in_specs += [pl.BlockSpec((2, tm, PEER_DKEY), lambda s, pt: (s % PEER_HEADS, s // PEER_HEADS, 0)),
                     pl.BlockSpec((2, PEER_NKEYS, PEER_DKEY), lambda s, pt: (s % PEER_HEADS, 0, 0))]
        args += [qh3, sub3]
        out_specs += [pl.BlockSpec((tm, n_slots), lambda s, pt: (s // PEER_HEADS, 0))] * 3
        out_shape += [jax.ShapeDtypeStruct((n_tok, n_slots), F32)] * 3
        scratch += _route_scratch(tm)
    grid_spec = pltpu.PrefetchScalarGridSpec(
        num_scalar_prefetch=1,
        grid=(n_seq,),
        in_specs=in_specs,
        out_specs=out_specs,
        scratch_shapes=scratch,
    )
    outs = pl.pallas_call(
        functools.partial(_moba_sample_body, n_pages=n_pages, page_size=page_size, n_new=n_new,
                          route=route is not None),
        grid_spec=grid_spec,
        out_shape=out_shape,
        compiler_params=_params(1),
        name="moba_sample_route" if route is not None else "moba_sample",
    )(page_table, *args)
    return outs[0], (tuple(outs[1:]) if route is not None else None)


def _memkv_body(mem_ref, mn_ref, w_ref, mkn_ref, k_ref, v_ref):
    xn = _rms(mem_ref[...], mn_ref[...]).astype(BF16)
    kv = jnp.dot(xn, w_ref[...], preferred_element_type=F32)
    mkn = mkn_ref[...]
    for h in range(MEM_HEADS):
        sl = slice(h * MEM_HEAD_DIM, (h + 1) * MEM_HEAD_DIM)
        k_ref[:, sl] = _rms(kv[:, sl], mkn)
    v_ref[...] = kv[:, D_MEM:]


def _memkv(mem2, mem_norm, w_mem_kv, mk_norm):
    n_rows = mem2.shape[0]
    tm = N_MEM
    spec = pl.BlockSpec((tm, D_MEM), lambda i: (i, 0))
    return pl.pallas_call(
        _memkv_body,
        grid=(n_rows // tm,),
        in_specs=[pl.BlockSpec((tm, D_MODEL), lambda i: (i, 0)), _resident((1, D_MODEL)),
                  _resident((D_MODEL, 2 * D_MEM)), _resident((1, MEM_HEAD_DIM))],
        out_specs=(spec, spec),
        out_shape=(jax.ShapeDtypeStruct((n_rows, D_MEM), F32),) * 2,
        compiler_params=_params(1),
        name="memkv",
    )(mem2, mem_norm, w_mem_kv, mk_norm)


def _memattn_body(mq_ref, k_ref, v_ref, o_ref, *, groups, rows, heads_on_lanes):
    scale = MEM_HEAD_DIM ** -0.5
    for g in range(groups):
        rs = slice(g * rows, (g + 1) * rows)
        for h in range(MEM_HEADS):
            sl = slice(h * MEM_HEAD_DIM, (h + 1) * MEM_HEAD_DIM)
            if heads_on_lanes:
                kh, vh = k_ref[g, :, sl], v_ref[g, :, sl]
            else:
                kh, vh = k_ref[g, :, h, :], v_ref[g, :, h, :]
            qh = mq_ref[rs, sl].astype(BF16)
            s = _nt_dot(qh, kh.astype(BF16)) * scale
            p = jnp.exp(s - jnp.max(s, axis=1, keepdims=True))
            l = jnp.sum(p, axis=1, keepdims=True)
            o = jnp.dot(p.astype(BF16), vh.astype(BF16), preferred_element_type=F32)
            o_ref[rs, sl] = (o / l).astype(o_ref.dtype)


def _memattn(mq2, mem_k, mem_v, mem_spec, *, groups, rows):
    n_tok = mq2.shape[0]
    tm = groups * rows
    row_spec = pl.BlockSpec((tm, D_MEM), lambda i: (i, 0))
    heads_on_lanes = sum(d is not None for d in mem_spec.block_shape) == 3
    return pl.pallas_call(
        functools.partial(_memattn_body, groups=groups, rows=rows, heads_on_lanes=heads_on_lanes),
        grid=(n_tok // tm,),
        in_specs=[row_spec, mem_spec, mem_spec],
        out_specs=row_spec,
        out_shape=jax.ShapeDtypeStruct((n_tok, D_MEM), BF16),
        compiler_params=_params(1),
        name="memattn",
    )(mq2, mem_k, mem_v)


def _post_body(x_ref, cy_ref, o_ref, om_ref, gate_ref, wco_ref, wao_ref, wmo_ref, wo_ref,
               fn_ref, wq_ref, h_ref, xn_ref, qh_ref):
    conv_out = jnp.dot(cy_ref[...], wco_ref[...], preferred_element_type=F32)
    attn_out = jnp.dot(o_ref[...], wao_ref[...], preferred_element_type=F32)
    mem_out = jnp.dot(om_ref[...], wmo_ref[...], preferred_element_type=F32)
    merged = (jax.nn.sigmoid(gate_ref[:, 0:D_MODEL]) * conv_out
              + jax.nn.sigmoid(gate_ref[:, D_MODEL:2 * D_MODEL]) * attn_out
              + jax.nn.sigmoid(gate_ref[:, 2 * D_MODEL:3 * D_MODEL]) * mem_out)
    h = x_ref[...] + jnp.dot(merged.astype(BF16), wo_ref[...], preferred_element_type=F32)
    h_ref[...] = h
    xn = _rms(h, fn_ref[...]).astype(BF16)
    xn_ref[...] = xn
    qh = jnp.dot(xn, wq_ref[...], preferred_element_type=F32)
    for hc in range(2 * PEER_HEADS):
        qh_ref[hc] = qh[:, hc * PEER_DKEY:(hc + 1) * PEER_DKEY].astype(BF16)


def _post(x2, cy, o2, om, gates, wco, wao, wmo, wo, ffn_norm, wq):
    n_tok = x2.shape[0]
    tm = min(TOKEN_TILE, n_tok)

    def rows(width):
        return pl.BlockSpec((tm, width), lambda i: (i, 0))

    return pl.pallas_call(
        _post_body,
        grid=(n_tok // tm,),
        in_specs=[rows(D_MODEL), rows(D_CONV), rows(D_ATTN), rows(D_MEM), rows(3 * D_MODEL),
                  _resident(wco.shape), _resident(wao.shape), _resident(wmo.shape),
                  _resident(wo.shape), _resident((1, D_MODEL)), _resident(wq.shape)],
        out_specs=(rows(D_MODEL), rows(D_MODEL),
                   pl.BlockSpec((2 * PEER_HEADS, tm, PEER_DKEY), lambda i: (0, i, 0))),
        out_shape=(jax.ShapeDtypeStruct((n_tok, D_MODEL), F32),
                   jax.ShapeDtypeStruct((n_tok, D_MODEL), BF16),
                   jax.ShapeDtypeStruct((2 * PEER_HEADS, n_tok, PEER_DKEY), BF16)),
        compiler_params=_params(1),
        name="post",
    )(x2, cy, o2, om, gates, wco, wao, wmo, wo, ffn_norm, wq)


_CAND_PAIRS = tuple((a, b) for a in range(PEER_TOPK) for b in range(PEER_TOPK)
                    if (a + 1) * (b + 1) <= PEER_TOPK)


def _tree_argmax(val, idx):
    chunk = 2 * SUBLANES
    if val.shape[0] > chunk:
        parts = [_tree_argmax(val[c:c + chunk], idx[c:c + chunk])
                 for c in range(0, val.shape[0], chunk)]
        val = jnp.stack([p[0] for p in parts])
        idx = jnp.stack([p[1] for p in parts])
    while val.shape[0] > 1:
        n = val.shape[0]
        pairs = n // 2
        v2 = val[:2 * pairs].reshape((pairs, 2) + val.shape[1:])
        i2 = idx[:2 * pairs].reshape((pairs, 2) + idx.shape[1:])
        left = v2[:, 0] >= v2[:, 1]
        new_val = jnp.maximum(v2[:, 0], v2[:, 1])
        new_idx = jnp.where(left, i2[:, 0], i2[:, 1])
        if n % 2:
            new_val = jnp.concatenate([new_val, val[2 * pairs:]], axis=0)
            new_idx = jnp.concatenate([new_idx, idx[2 * pairs:]], axis=0)
        val, idx = new_val, new_idx
    return val[0], idx[0]


def _oddeven_merge(lo, hi, r):
    step = r * 2
    if step < hi - lo:
        yield from _oddeven_merge(lo, hi, step)
        yield from _oddeven_merge(lo + r, hi, step)
        yield from [(i, i + r) for i in range(lo + r, hi - r, step)]
    else:
        yield (lo, lo + r)


def _oddeven_merge_sort(lo, hi):
    if hi - lo >= 1:
        mid = lo + (hi - lo) // 2
        yield from _oddeven_merge_sort(lo, mid)
        yield from _oddeven_merge_sort(mid + 1, hi)
        yield from _oddeven_merge(lo, hi, 1)


_SORT_K = tuple(_oddeven_merge_sort(0, PEER_TOPK - 1))


def _top_k_sorted(vals, ids):
    k = PEER_TOPK
    n = vals.shape[0]
    assert n % k == 0 and (n // k) & (n // k - 1) == 0

    def exchange(v, i, a, b, keep_low=True):
        first = (v[a] > v[b]) | ((v[a] == v[b]) & (i[a] < i[b]))
        hi_i = jnp.where(first, i[a], i[b])
        if keep_low:
            lo_i = jnp.where(first, i[b], i[a])
            v[a], v[b] = jnp.maximum(v[a], v[b]), jnp.minimum(v[a], v[b])
            i[a], i[b] = hi_i, lo_i
        else:
            v[a], i[a] = jnp.maximum(v[a], v[b]), hi_i

    groups = []
    for g in range(n // k):
        v = [vals[g * k + r] for r in range(k)]
        i = [ids[g * k + r] for r in range(k)]
        for a, b in _SORT_K:
            exchange(v, i, a, b)
        groups.append((v, i))
    while len(groups) > 1:
        merged = []
        for (va, ia), (vb, ib) in zip(groups[0::2], groups[1::2]):
            v = va + vb
            i = ia + ib
            for r in range(k):
                exchange(v, i, r, 2 * k - 1 - r, keep_low=False)
            v, i = v[:k], i[:k]
            d = k // 2
            while d >= 1:
                for r in range(k):
                    if r & d == 0:
                        exchange(v, i, r, r + d)
                d //= 2
            merged.append((v, i))
        groups = merged
    return groups[0]


def _route_scratch(tm):
    tile = (tm // LANES, LANES)
    keys = pltpu.VMEM((PEER_NKEYS,) + tile, F32)
    small = pltpu.VMEM((PEER_TOPK,) + tile, F32)
    slots = pltpu.VMEM((PEER_HEADS * PEER_TOPK,) + tile, F32)
    return [keys, keys, small, small, small, small, small, small, slots, slots, slots]


def _route_head(h, qh1_ref, qh2_ref, sub1_ref, sub2_ref, scratch):
    (x1_ref, x2_ref, st1_ref, ix1_ref, st2_ref, ix2_ref, best_ref, flat_ref,
     si_ref, sj_ref, sg_ref) = scratch
    n_sub = x1_ref.shape[1]
    k = PEER_TOPK
    tile = (n_sub, LANES)
    key_id = lax.broadcasted_iota(jnp.int32, (PEER_NKEYS,) + tile, 0).astype(F32)
    flat_id = jnp.stack([jnp.full(tile, float(a * k + b), F32) for a, b in _CAND_PAIRS])

    def scores_into(sub_ref, qh_ref, x_ref):
        scores = _nt_dot(sub_ref[...], qh_ref[...])
        for s in range(n_sub):
            x_ref[:, s, :] = scores[:, s * LANES:(s + 1) * LANES]

    def top_keys(x_ref, st_ref, ix_ref):
        v, i = _top_k_sorted(x_ref[...], key_id)
        st_ref[...] = jnp.stack(v)
        ix_ref[...] = jnp.stack(i)

    def lookup(rank, table):
        out = jnp.zeros(rank.shape, F32)
        for a in range(k):
            out = jnp.where(rank == float(a), table[a][None], out)
        return out

    scores_into(sub1_ref, qh1_ref, x1_ref)
    scores_into(sub2_ref, qh2_ref, x2_ref)
    top_keys(x1_ref, st1_ref, ix1_ref)
    top_keys(x2_ref, st2_ref, ix2_ref)
    st1 = st1_ref[...]
    st2 = st2_ref[...]
    cand = jnp.stack([st1[a] + st2[b] for a, b in _CAND_PAIRS])

    def rnd2(r, x):
        m, f = _tree_argmax(x, flat_id)
        best_ref[r] = m
        flat_ref[r] = f
        return jnp.where(flat_id == f[None], NEG_INF, x)

    lax.fori_loop(0, k, rnd2, cand)
    best = best_ref[...]
    flat = flat_ref[...]
    rank_a = jnp.floor(flat * (1.0 / k))
    rank_b = flat - rank_a * k
    e = jnp.exp(best - best[0:1])
    rows = pl.ds(pl.multiple_of(h * k, k), k)
    si_ref[rows] = lookup(rank_a, ix1_ref[...])
    sj_ref[rows] = lookup(rank_b, ix2_ref[...])
    sg_ref[rows] = e / jnp.sum(e, axis=0, keepdims=True)


def _route_emit(scratch, i_ref, j_ref, g_ref):
    si_ref, sj_ref, sg_ref = scratch[-3:]
    for s in range(si_ref.shape[1]):
        rows = slice(s * LANES, (s + 1) * LANES)
        i_ref[rows, :] = si_ref[:, s, :].T
        j_ref[rows, :] = sj_ref[:, s, :].T
        g_ref[rows, :] = sg_ref[:, s, :].T


def _route_body(qh_ref, sub_ref, i_ref, j_ref, g_ref, *scratch):
    def head(h, carry):
        _route_head(h, qh_ref.at[2 * h], qh_ref.at[2 * h + 1], sub_ref.at[2 * h],
                    sub_ref.at[2 * h + 1], scratch)
        return carry

    lax.fori_loop(0, PEER_HEADS, head, 0)
    _route_emit(scratch, i_ref, j_ref, g_ref)


def _route(qh3, subkeys):
    n_tok = qh3.shape[1]
    tm = min(ROUTE_TILE, n_tok)
    n_slots = PEER_HEADS * PEER_TOPK
    assert n_slots == LANES and tm % LANES == 0 and n_tok % tm == 0
    sub3 = subkeys.reshape(2 * PEER_HEADS, PEER_NKEYS, PEER_DKEY).astype(BF16)
    slot_spec = pl.BlockSpec((tm, n_slots), lambda i: (i, 0))
    return pl.pallas_call(
        _route_body,
        grid=(n_tok // tm,),
        in_specs=[pl.BlockSpec((2 * PEER_HEADS, tm, PEER_DKEY), lambda i: (0, i, 0)),
                  _resident(sub3.shape)],
        out_specs=(slot_spec,) * 3,
        out_shape=(jax.ShapeDtypeStruct((n_tok, n_slots), F32),) * 3,
        scratch_shapes=_route_scratch(tm),
        compiler_params=_params(1),
        name="peer_route",
    )(qh3, sub3)


WBUILD_GROUP = 16
WBUILD_PITCH = PEER_NKEYS + SUBLANES


def _wbuild_body(i_ref, j_ref, g_ref, w_ref, stage_a_ref, stage_b_ref):
    tm = i_ref.shape[0]
    key = lax.broadcasted_iota(jnp.int32, (PEER_NKEYS, LANES), 0).astype(F32)

    def build(gi, stage_ref):
        base = pl.multiple_of(gi * WBUILD_GROUP, WBUILD_GROUP)
        for tt in range(WBUILD_GROUP):
            irow = i_ref[pl.ds(base + tt, 1), :]
            jrow = j_ref[pl.ds(base + tt, 1), :]
            grow = g_ref[pl.ds(base + tt, 1), :]
            a = jnp.where(key == irow, grow, 0.0).astype(BF16)
            b = jnp.where(key == jrow, 1.0, 0.0).astype(BF16)
            stage_ref[tt * WBUILD_PITCH:tt * WBUILD_PITCH + PEER_NKEYS, :] = _nt_dot(a, b)

    def drain(gi, stage_ref):
        base = pl.multiple_of(gi * WBUILD_GROUP, WBUILD_GROUP)
        for i in range(PEER_NKEYS):
            tile = stage_ref[pl.ds(i, WBUILD_GROUP, stride=WBUILD_PITCH), :]
            w_ref[pl.ds(base, WBUILD_GROUP), i * PEER_NKEYS:(i + 1) * PEER_NKEYS] = tile.astype(BF16)

    n_groups = tm // WBUILD_GROUP
    assert n_groups % 2 == 0
    build(0, stage_a_ref)

    def step(p, carry):
        build(2 * p + 1, stage_b_ref)
        drain(2 * p, stage_a_ref)
        build(2 * p + 2, stage_a_ref)
        drain(2 * p + 1, stage_b_ref)
        return carry

    lax.fori_loop(0, n_groups // 2 - 1, step, 0)
    build(n_groups - 1, stage_b_ref)
    drain(n_groups - 2, stage_a_ref)
    drain(n_groups - 1, stage_b_ref)


def _wbuild(si, sj, sg):
    n_tok = si.shape[0]
    tm = min(WBUILD_TILE, n_tok)
    assert tm % WBUILD_GROUP == 0
    slot_spec = pl.BlockSpec((tm, LANES), lambda i: (i, 0))
    return pl.pallas_call(
        _wbuild_body,
        grid=(n_tok // tm,),
        in_specs=[slot_spec] * 3,
        out_specs=pl.BlockSpec((tm, PEER_N_EXPERTS), lambda i: (i, 0)),
        out_shape=jax.ShapeDtypeStruct((n_tok, PEER_N_EXPERTS), BF16),
        scratch_shapes=[pltpu.VMEM((WBUILD_GROUP * WBUILD_PITCH, PEER_NKEYS), F32)] * 2,
        compiler_params=_params(1),
        name="peer_wbuild",
    )(si, sj, sg)


def _peer_body(xn_ref, u_ref, v_ref, w_ref, h_ref, y_ref):
    @pl.when(pl.program_id(1) == 0)
    def _():
        y_ref[...] = h_ref[...]

    xn = xn_ref[...]
    sub = PEER_EXPERT_SUB
    for s in range(u_ref.shape[0] // sub):
        sl = slice(s * sub, (s + 1) * sub)
        z = _nt_dot(xn, u_ref[sl, :])
        a = 0.5 * z * (1.0 + lax.erf(z * (2.0 ** -0.5)))
        a = a * w_ref[:, sl].astype(F32)
        y_ref[...] += jnp.dot(a.astype(BF16), v_ref[sl, :], preferred_element_type=F32)


def _peer_dense(xn, w2, h, peer_u, peer_v):
    n_tok = xn.shape[0]
    tm = min(PEER_TOKEN_TILE, n_tok)
    ec = PEER_EXPERT_CHUNK
    assert n_tok % tm == 0 and PEER_N_EXPERTS % ec == 0 and ec % PEER_EXPERT_SUB == 0
    tok_spec = pl.BlockSpec((tm, D_MODEL), lambda i, c: (i, 0))
    exp_spec = pl.BlockSpec((ec, D_MODEL), lambda i, c: (c, 0))
    return pl.pallas_call(
        _peer_body,
        grid=(n_tok // tm, PEER_N_EXPERTS // ec),
        in_specs=[tok_spec, exp_spec, exp_spec, pl.BlockSpec((tm, ec), lambda i, c: (i, c)),
                  pl.BlockSpec((tm, D_MODEL), lambda i, c: (i, 0), pipeline_mode=pl.Buffered(1))],
        out_specs=tok_spec,
        out_shape=jax.ShapeDtypeStruct((n_tok, D_MODEL), F32),
        compiler_params=_params(2),
        name="peer_dense",
    )(xn, peer_u, peer_v, w2, h)


def _row(v):
    return v.reshape(1, -1)


def _peer(h, xn, qh3, lw, slots=None):
    if slots is None:
        slots = _route(qh3, lw["subkeys"])
    w2 = _wbuild(*slots)
    return _peer_dense(xn, w2, h, lw["peer_u"], lw["peer_v"])


def kernel(x_prompt, x_sample, mem_prompt, cache_k, cache_v, cache_conv, cache_mem_k, cache_mem_v,
           page_table, attn_norm, w_in, conv_w, w_conv_out, q_norm, k_norm, w_attn_out,
           mem_norm, w_mem_kv, mk_norm, mq_norm, w_mem_out, w_o,
           ffn_norm, peer_wq, peer_subkeys, peer_u, peer_v):
    n_p, t_p, _ = x_prompt.shape
    n_s, t_s, _ = x_sample.shape
    depth = attn_norm.shape[0]
    past_len = page_table.shape[1] * cache_k.shape[2]
    h_p = x_prompt.reshape(n_p * t_p, D_MODEL)
    h_s = x_sample.reshape(n_s * t_s, D_MODEL)
    outs = [[] for _ in range(8)]
    for l in range(depth):
        lw = dict(
            wco=w_conv_out[l].astype(BF16), wao=w_attn_out[l].astype(BF16),
            wmo=w_mem_out[l].astype(BF16), wo=w_o[l].astype(BF16), wq=peer_wq[l].astype(BF16),
            ffn_norm=_row(ffn_norm[l]), subkeys=peer_subkeys[l],
            peer_u=peer_u[l].astype(BF16), peer_v=peer_v[l].astype(BF16))
        in_w = (_row(attn_norm[l]), w_in[l].astype(BF16), _row(q_norm[l]), _row(k_norm[l]),
                _row(mq_norm[l]), conv_w[l])

        mem_k, mem_v = _memkv(mem_prompt.reshape(n_p * N_MEM, D_MODEL), _row(mem_norm[l]),
                              w_mem_kv[l].astype(BF16), _row(mk_norm[l]))
        zero_state = jnp.zeros((SUBLANES, D_CONV), F32)
        cy, u, q, k, v, mq, gates, kmean = _inproj(h_p, *in_w, zero_state, zero_state, seq_len=t_p)
        o = _moba_prompt(q, k, v, kmean, n_seq=n_p, seq_len=t_p)
        steps_per_mem = t_p // TOKEN_TILE

        prompt_mem = pl.BlockSpec((1, N_MEM, D_MEM), lambda i: (i // steps_per_mem, 0, 0))
        om = _memattn(mq, mem_k.reshape(n_p, N_MEM, D_MEM), mem_v.reshape(n_p, N_MEM, D_MEM),
                      prompt_mem, groups=1, rows=TOKEN_TILE)
        h_mid_p, xn_p, qh3_p = _post(h_p, cy, o, om, gates, lw["wco"], lw["wao"], lw["wmo"], lw["wo"],
                                     lw["ffn_norm"], lw["wq"])
        outs[0].append(k.reshape(n_p, t_p, N_HEADS, HEAD_DIM))
        outs[1].append(v.reshape(n_p, t_p, N_HEADS, HEAD_DIM))
        outs[2].append(u.reshape(n_p, t_p, D_CONV)[:, t_p - (CONV_W - 1):])
        outs[3].append(mem_k.reshape(n_p, N_MEM, MEM_HEADS, MEM_HEAD_DIM))
        outs[4].append(mem_v.reshape(n_p, N_MEM, MEM_HEADS, MEM_HEAD_DIM))

        state = cache_conv[l]
        s0x = jnp.repeat(state[:, 0], t_s, axis=0)
        s1x = jnp.repeat(state[:, 1], t_s, axis=0)
        cy, u, q, k, v, mq, gates, _ = _inproj(h_s, *in_w, s0x, s1x, seq_len=t_s)
        k4 = k.reshape(n_s, t_s, N_HEADS, HEAD_DIM)
        v4 = v.reshape(n_s, t_s, N_HEADS, HEAD_DIM)
        rows = n_s * t_s * N_HEADS
        ride = (n_p * t_p) % ROUTE_TILE == 0 and (n_p * t_p // ROUTE_TILE) * PEER_HEADS == n_s
        o, slots_p = _moba_sample(q.reshape(rows, HEAD_DIM), k4.reshape(rows, HEAD_DIM),
                                  v4.reshape(rows, HEAD_DIM), cache_k, cache_v, page_table, l,
                                  n_seq=n_s, n_new=t_s, past_len=past_len,
                                  route=(qh3_p, lw["subkeys"]) if ride else None)
        h_p = _peer(h_mid_p, xn_p, qh3_p, lw, slots=slots_p)
        o = o.reshape(n_s * t_s, D_ATTN)
        groups = SUBLANES

        sample_mem = pl.BlockSpec((None, groups, N_MEM, MEM_HEADS, MEM_HEAD_DIM),
                                  lambda i: (l, i, 0, 0, 0))
        om = _memattn(mq, cache_mem_k, cache_mem_v, sample_mem, groups=groups, rows=t_s)
        h_mid_s, xn_s, qh3_s = _post(h_s, cy, o, om, gates, lw["wco"], lw["wao"], lw["wmo"], lw["wo"],
                                     lw["ffn_norm"], lw["wq"])
        h_s = _peer(h_mid_s, xn_s, qh3_s, lw)
        outs[5].append(k4)
        outs[6].append(v4)
        ext = jnp.concatenate([state, u.reshape(n_s, t_s, D_CONV)], axis=1)
        outs[7].append(ext[:, t_s:])
    stacked = [jnp.stack(o) for o in outs]
    return (h_p.reshape(n_p, t_p, D_MODEL), h_s.reshape(n_s, t_s, D_MODEL), *stacked)
```

```python
import functools

import jax
import jax.numpy as jnp
from jax import lax
from jax.experimental import pallas as pl
from jax.experimental.pallas import tpu as pltpu

F32 = jnp.float32
BF16 = jnp.bfloat16
HIGHEST = lax.Precision.HIGHEST
NEG_INF = float("-inf")
LOG2_E = 1.4426950408889634

SUBLANES = 8
LANES = 128
VMEM_LIMIT_BYTES = 56 * 1024 * 1024

EPS = 1e-6
D_MODEL = 1024
D_CONV = 512
CONV_W = 3
N_HEADS = 8
HEAD_DIM = 128
D_ATTN = N_HEADS * HEAD_DIM
MOBA_BLOCK = 256
MOBA_TOPK = 3
N_MEM = 256
MEM_HEADS = 4
MEM_HEAD_DIM = 128
D_MEM = MEM_HEADS * MEM_HEAD_DIM
PEER_HEADS = 8
PEER_NKEYS = 128
PEER_DKEY = 128
PEER_TOPK = 16
PEER_N_EXPERTS = PEER_NKEYS * PEER_NKEYS
D_PEER_Q = PEER_HEADS * 2 * PEER_DKEY
D_IN = 3 * D_CONV + 3 * D_ATTN + D_MEM + 3 * D_MODEL
OFF_CB, OFF_CC, OFF_CH = 0, D_CONV, 2 * D_CONV
OFF_Q = 3 * D_CONV
OFF_K = OFF_Q + D_ATTN
OFF_V = OFF_K + D_ATTN
OFF_MQ = OFF_V + D_ATTN
OFF_G = OFF_MQ + D_MEM

TOKEN_TILE = 256
MOBA_HEADS_PER_STEP = 2
ROUTE_TILE = 1024
WBUILD_TILE = 512
PEER_TOKEN_TILE = 1024
PEER_EXPERT_CHUNK = 2048
PEER_EXPERT_SUB = 1024


def _nt_dot(a, b, precision=None):
    return lax.dot_general(a, b, (((1,), (1,)), ((), ())), precision=precision,
                           preferred_element_type=F32)


def _params(n_grid_dims):
    return pltpu.CompilerParams(dimension_semantics=("arbitrary",) * n_grid_dims,
                                vmem_limit_bytes=VMEM_LIMIT_BYTES)


def _rms(x, gain):
    r = lax.rsqrt(jnp.mean(x * x, axis=-1, keepdims=True) + EPS)
    return x * r * gain


def _resident(shape):
    nd = len(shape)
    return pl.BlockSpec(shape, lambda *_: (0,) * nd, pipeline_mode=pl.Buffered(1))


def _inproj_body(x_ref, an_ref, w_ref, qn_ref, kn_ref, mqn_ref, cw_ref, s0_ref, s1_ref,
                 cy_ref, u_ref, q_ref, k_ref, v_ref, mq_ref, gate_ref, kmean_ref,
                 ubuf_ref, *, tiles_per_seq, seq_len):
    tm = x_ref.shape[0]
    xn = _rms(x_ref[...], an_ref[...]).astype(BF16)

    def proj(off, width):
        return jnp.dot(xn, w_ref[:, off:off + width], preferred_element_type=F32)

    cb = proj(OFF_CB, D_CONV)
    u = proj(OFF_CC, D_CONV) * proj(OFF_CH, D_CONV)
    u_ref[...] = u
    if seq_len >= tm:
        @pl.when(pl.program_id(0) % tiles_per_seq == 0)
        def _():
            ubuf_ref[0:SUBLANES, :] = jnp.zeros((SUBLANES, D_CONV), F32)
    else:
        ubuf_ref[0:SUBLANES, :] = jnp.zeros((SUBLANES, D_CONV), F32)
    ubuf_ref[SUBLANES:SUBLANES + tm, :] = u
    um1 = ubuf_ref[SUBLANES - 1:SUBLANES - 1 + tm, :]
    um2 = ubuf_ref[SUBLANES - 2:SUBLANES - 2 + tm, :]
    if seq_len >= tm:
        ubuf_ref[0:SUBLANES, :] = ubuf_ref[tm:tm + SUBLANES, :]
    else:
        assert seq_len == SUBLANES
        tl = lax.broadcasted_iota(jnp.int32, (tm, D_CONV), 0) % seq_len
        s0 = s0_ref[...]
        s1 = s1_ref[...]
        um1 = jnp.where(tl == 0, s1, um1)
        um2 = jnp.where(tl == 0, s0, jnp.where(tl == 1, s1, um2))
    cw = cw_ref[...]
    y = cw[0:1, :] * um2 + cw[1:2, :] * um1 + cw[2:3, :] * u
    cy_ref[...] = (cb * y).astype(BF16)

    qf = proj(OFF_Q, D_ATTN)
    kf = proj(OFF_K, D_ATTN)
    qn = qn_ref[...]
    kn = kn_ref[...]
    for h in range(N_HEADS):
        sl = slice(h * HEAD_DIM, (h + 1) * HEAD_DIM)
        q_ref[:, sl] = _rms(qf[:, sl], qn)
        k_ref[:, sl] = _rms(kf[:, sl], kn)
    kmean_ref[0] = jnp.mean(k_ref[...], axis=0, keepdims=True)
    v_ref[...] = proj(OFF_V, D_ATTN)
    mqf = proj(OFF_MQ, D_MEM)
    mqn = mqn_ref[...]
    for h in range(MEM_HEADS):
        sl = slice(h * MEM_HEAD_DIM, (h + 1) * MEM_HEAD_DIM)
        mq_ref[:, sl] = _rms(mqf[:, sl], mqn)
    gate_ref[...] = proj(OFF_G, 3 * D_MODEL)


def _inproj(x2, attn_norm, w_in, q_norm, k_norm, mq_norm, conv_w, s0x, s1x, *, seq_len):
    n_tok = x2.shape[0]
    tm = min(TOKEN_TILE, n_tok)
    assert n_tok % tm == 0 and (seq_len % tm == 0 or seq_len == SUBLANES)
    n_tiles = n_tok // tm
    tiles_per_seq = max(seq_len // tm, 1)
    long_seq = seq_len >= tm

    def rows(width):
        return pl.BlockSpec((tm, width), lambda i: (i, 0))

    state_spec = _resident(s0x.shape) if long_seq else rows(D_CONV)
    out_shape = (
        jax.ShapeDtypeStruct((n_tok, D_CONV), BF16),
        jax.ShapeDtypeStruct((n_tok, D_CONV), F32),
        jax.ShapeDtypeStruct((n_tok, D_ATTN), F32),
        jax.ShapeDtypeStruct((n_tok, D_ATTN), F32),
        jax.ShapeDtypeStruct((n_tok, D_ATTN), F32),
        jax.ShapeDtypeStruct((n_tok, D_MEM), F32),
        jax.ShapeDtypeStruct((n_tok, 3 * D_MODEL), F32),
        jax.ShapeDtypeStruct((n_tiles, 1, D_ATTN), F32),
    )
    return pl.pallas_call(
        functools.partial(_inproj_body, tiles_per_seq=tiles_per_seq, seq_len=seq_len),
        grid=(n_tiles,),
        in_specs=[rows(D_MODEL), _resident((1, D_MODEL)), _resident((D_MODEL, D_IN)),
                  _resident((1, HEAD_DIM)), _resident((1, HEAD_DIM)), _resident((1, MEM_HEAD_DIM)),
                  _resident((CONV_W, D_CONV)), state_spec, state_spec],
        out_specs=(rows(D_CONV), rows(D_CONV), rows(D_ATTN), rows(D_ATTN), rows(D_ATTN),
                   rows(D_MEM), rows(3 * D_MODEL),
                   pl.BlockSpec((1, 1, D_ATTN), lambda i: (i, 0, 0))),
        out_shape=out_shape,
        scratch_shapes=[pltpu.VMEM((tm + 2 * SUBLANES, D_CONV), F32)],
        compiler_params=_params(1),
        name="inproj",
    )(x2, attn_norm, w_in, q_norm, k_norm, mq_norm, conv_w, s0x, s1x)


def _moba_prompt_body(q_ref, k_ref, v_ref, km_ref, o_ref, kb_ref, vt_ref, sc_ref, m_ref,
                      *, n_blocks):
    s_id = pl.program_id(2)
    blk = MOBA_BLOCK
    scale = HEAD_DIM ** -0.5
    heads = kb_ref.shape[0]

    @pl.when(s_id == 0)
    def _():
        for hh in range(heads):
            lanes = slice(hh * HEAD_DIM, (hh + 1) * HEAD_DIM)
            kb_ref[hh] = k_ref[:, lanes].astype(BF16)
            for j in range(n_blocks):
                vt_ref[hh, :, j * blk:(j + 1) * blk] = v_ref[j * blk:(j + 1) * blk, lanes].T.astype(BF16)

    key_row = lax.broadcasted_iota(jnp.int32, (blk, blk), 0)
    query_col = lax.broadcasted_iota(jnp.int32, (blk, blk), 1)

    def score_block(own, hh):
        lanes = slice(hh * HEAD_DIM, (hh + 1) * HEAD_DIM)
        q = q_ref[:, lanes]
        gate = _nt_dot(km_ref[0, :, lanes], q, precision=HIGHEST)
        blk_id = lax.broadcasted_iota(jnp.int32, gate.shape, 0)
        rank = jnp.zeros(gate.shape, F32)
        for jp in range(own):
            other = gate[jp:jp + 1, :]
            beats = (other > gate) | ((other == gate) & (jp < blk_id))
            rank = rank + jnp.where(beats, 1.0, 0.0)
        k_sel = min(MOBA_TOPK, n_blocks)
        sel = jnp.where((rank < k_sel) & (blk_id < own), 1.0, 0.0)
        qt = (q * (scale * LOG2_E)).T.astype(BF16)
        buf = own % 2
        m = None
        for j in range(own + 1):
            s = jnp.dot(kb_ref[hh, j * blk:(j + 1) * blk, :], qt, preferred_element_type=F32)
            if j == own:
                s = jnp.where(key_row <= query_col, s, NEG_INF)
            else:
                s = jnp.where(sel[j:j + 1, :] > 0.5, s, NEG_INF)
            sc_ref[hh, buf, j] = s
            mj = jnp.max(s, axis=0, keepdims=True)
            m = mj if m is None else jnp.maximum(m, mj)
        m_ref[hh, buf] = m

    def finish_block(own, hh):
        buf = own % 2
        m = m_ref[hh, buf]
        l = jnp.zeros((1, blk), F32)
        acc = jnp.zeros((HEAD_DIM, blk), F32)
        for j in range(own + 1):
            p = jnp.exp2(sc_ref[hh, buf, j] - m)
            l = l + jnp.sum(p, axis=0, keepdims=True)
            acc = acc + jnp.dot(vt_ref[hh, :, j * blk:(j + 1) * blk], p.astype(BF16),
                                preferred_element_type=F32)
        o_ref[:, hh * HEAD_DIM:(hh + 1) * HEAD_DIM] = (acc / l).T.astype(o_ref.dtype)

    for step in range(n_blocks + 1):
        @pl.when(s_id == step)
        def _(step=step):
            for hh in range(heads):
                if step < n_blocks:
                    score_block(step, hh)
                if step >= 1:
                    finish_block(step - 1, hh)


def _moba_prompt(q2, k2, v2, kmean, *, n_seq, seq_len):
    blk = MOBA_BLOCK
    assert seq_len % blk == 0 and TOKEN_TILE == blk
    nb = seq_len // blk
    km = kmean.reshape(n_seq, nb, D_ATTN)
    hs = MOBA_HEADS_PER_STEP
    width = hs * HEAD_DIM
    assert N_HEADS % hs == 0
    kv_spec = pl.BlockSpec((seq_len, width), lambda n, h, s: (n, h))
    return pl.pallas_call(
        functools.partial(_moba_prompt_body, n_blocks=nb),
        grid=(n_seq, N_HEADS // hs, nb + 1),
        in_specs=[pl.BlockSpec((blk, width), lambda n, h, s: (n * nb + jnp.minimum(s, nb - 1), h)),
                  kv_spec, kv_spec,
                  pl.BlockSpec((1, nb, width), lambda n, h, s: (n, 0, h))],
        out_specs=pl.BlockSpec((blk, width), lambda n, h, s: (n * nb + jnp.maximum(s - 1, 0), h)),
        out_shape=jax.ShapeDtypeStruct(q2.shape, BF16),
        scratch_shapes=[pltpu.VMEM((hs, seq_len, HEAD_DIM), BF16),
                        pltpu.VMEM((hs, HEAD_DIM, seq_len), BF16),
                        pltpu.VMEM((hs, 2, nb, blk, blk), F32), pltpu.VMEM((hs, 2, 1, blk), F32)],
        compiler_params=_params(3),
        name="moba_prompt",
    )(q2, k2, v2, km)


def _moba_sample_body(pt_ref, q_ref, kn_ref, vn_ref, *rest, n_pages, page_size, n_new, route):
    del pt_ref
    kp = rest[:n_pages]
    vp = rest[n_pages:2 * n_pages]
    rest = rest[2 * n_pages:]
    if route:
        qh_ref, sub_ref, o_ref, i_ref, j_ref, g_ref, s_ref = rest[:7]
        route_scratch = rest[7:]
    else:
        o_ref, s_ref = rest
    rows = n_new * N_HEADS
    page_rows = page_size * N_HEADS
    pages_per_block = MOBA_BLOCK // page_size
    n_past_blocks = n_pages // pages_per_block
    scale = HEAD_DIM ** -0.5

    q = q_ref[...]
    qb = (q * (scale * LOG2_E)).astype(BF16)
    kmeans = []
    for blk in range(n_past_blocks):
        acc = jnp.zeros((N_HEADS, HEAD_DIM), F32)
        for p in range(blk * pages_per_block, (blk + 1) * pages_per_block):
            part = kp[p][...]
            s_ref[:, p * page_rows:(p + 1) * page_rows] = _nt_dot(
                qb, part.reshape(page_rows, HEAD_DIM).astype(BF16))
            while part.shape[0] > 1:
                half = part.shape[0] // 2
                part = part[:half] + part[half:]
            acc = acc + part[0]
        kmeans.append(acc * (1.0 / MOBA_BLOCK))
    kmean = jnp.concatenate(kmeans, axis=0)
    gate = _nt_dot(q, kmean, precision=HIGHEST)
    r_head = lax.broadcasted_iota(jnp.int32, gate.shape, 0) % N_HEADS
    c_idx = lax.broadcasted_iota(jnp.int32, gate.shape, 1)
    c_blk = c_idx // N_HEADS
    same_head = (c_idx % N_HEADS) == r_head
    rank = jnp.zeros(gate.shape, F32)
    for bp in range(n_past_blocks):
        colval = jnp.max(jnp.where(same_head & (c_blk == bp), gate, NEG_INF), axis=1, keepdims=True)
        beats = (colval > gate) | ((colval == gate) & (bp < c_blk))
        rank = rank + jnp.where(beats, 1.0, 0.0)
    k_sel = min(MOBA_TOPK, n_past_blocks + 1)
    sel = jnp.where(same_head & (rank < k_sel), 1.0, 0.0)

    head_mask = N_HEADS - 1
    n_row = lax.broadcasted_iota(jnp.int32, (rows, LANES), 0)
    n_col = lax.broadcasted_iota(jnp.int32, (rows, LANES), 1)
    same_head_t = (n_col & head_mask) == (n_row & head_mask)
    m_t = jnp.full((rows, LANES), NEG_INF, F32)
    for blk in range(n_past_blocks):
        blk_sel = jnp.max(jnp.where(c_blk == blk, sel, 0.0), axis=1, keepdims=True) > 0.5
        bias_t = jnp.where(same_head_t & blk_sel, 0.0, NEG_INF)
        for t in range(blk * pages_per_block * page_rows // LANES,
                       (blk + 1) * pages_per_block * page_rows // LANES):
            s = s_ref[:, t * LANES:(t + 1) * LANES] + bias_t
            s_ref[:, t * LANES:(t + 1) * LANES] = s
            m_t = jnp.maximum(m_t, s)
    pad = jnp.zeros((LANES - rows, HEAD_DIM), F32)
    kn = jnp.concatenate([kn_ref[...], pad], axis=0).astype(BF16)
    vn = jnp.concatenate([vn_ref[...], pad], axis=0).astype(BF16)
    keep_new = same_head_t & (n_col // N_HEADS <= n_row // N_HEADS) & (n_col < rows)
    sn = jnp.where(keep_new, _nt_dot(qb, kn), NEG_INF)
    past = n_pages * page_rows
    s_ref[:, past:past + LANES] = sn
    m = jnp.max(jnp.maximum(m_t, sn), axis=1, keepdims=True)
    l = jnp.zeros((rows, 1), F32)
    acc = jnp.zeros((rows, HEAD_DIM), F32)
    for p in range(n_pages):
        pp = jnp.exp2(s_ref[:, p * page_rows:(p + 1) * page_rows] - m)
        l = l + jnp.sum(pp, axis=1, keepdims=True)
        vpage = vp[p][...].reshape(page_rows, HEAD_DIM).astype(BF16)
        acc = acc + jnp.dot(pp.astype(BF16), vpage, preferred_element_type=F32)
    pn = jnp.exp2(s_ref[:, past:past + LANES] - m)
    l = l + jnp.sum(pn, axis=1, keepdims=True)
    acc = acc + jnp.dot(pn.astype(BF16), vn, preferred_element_type=F32)
    o_ref[...] = (acc / l).astype(o_ref.dtype)

    if route:
        head = pl.program_id(0) % PEER_HEADS
        _route_head(head, qh_ref.at[0], qh_ref.at[1], sub_ref.at[0], sub_ref.at[1], route_scratch)

        @pl.when(head == PEER_HEADS - 1)
        def _():
            _route_emit(route_scratch, i_ref, j_ref, g_ref)


def _moba_sample(q8, kn8, vn8, cache_k, cache_v, page_table, layer, *, n_seq, n_new, past_len,
                 route=None):
    page_size = cache_k.shape[2]
    n_pages = page_table.shape[1]
    assert past_len == n_pages * page_size and past_len % MOBA_BLOCK == 0
    assert MOBA_BLOCK % page_size == 0 and n_new <= MOBA_BLOCK and n_new * N_HEADS <= LANES
    assert cache_k.shape[3:] == (N_HEADS, HEAD_DIM) and N_HEADS == SUBLANES
    assert page_size & (page_size - 1) == 0
    rows = n_new * N_HEADS
    row_spec = pl.BlockSpec((rows, HEAD_DIM), lambda s, pt: (s, 0))

    def page_spec(p):
        return pl.BlockSpec((None, None, page_size, N_HEADS, HEAD_DIM),
                            lambda s, pt: (layer, pt[s, p], 0, 0, 0))

    page_specs = [page_spec(p) for p in range(n_pages)]
    in_specs = [row_spec, row_spec, row_spec] + page_specs + page_specs
    args = [q8, kn8, vn8] + [cache_k] * n_pages + [cache_v] * n_pages
    out_specs = [row_spec]
    out_shape = [jax.ShapeDtypeStruct(q8.shape, BF16)]
    scratch = [pltpu.VMEM((rows, n_pages * page_size * N_HEADS + LANES), F32)]
    if route is not None:
        qh3, subkeys = route
        n_tok = qh3.shape[1]
        tm = ROUTE_TILE
        n_slots = PEER_HEADS * PEER_TOPK
        assert n_slots == LANES and n_tok % tm == 0 and (n_tok // tm) * PEER_HEADS == n_seq
        sub3 = subkeys.reshape(2 * PEER_HEADS, PEER_NKEYS, PEER_DKEY).astype(BF16)
        in_specs += [pl.BlockSpec((2, tm, PEER_DKEY), lambda s, pt: (s % PEER_HEADS, s // PEER_HEADS, 0)),
                     pl.BlockSpec((2, PEER_NKEYS, PEER_DKEY), lambda s, pt: (s % PEER_HEADS, 0, 0))]
        args += [qh3, sub3]
        out_specs += [pl.BlockSpec((tm, n_slots), lambda s, pt: (s // PEER_HEADS, 0))] * 3
        out_shape += [jax.ShapeDtypeStruct((n_tok, n_slots), F32)] * 3
        scratch += _route_scratch(tm)
    grid_spec = pltpu.PrefetchScalarGridSpec(
        num_scalar_prefetch=1,
        grid=(n_seq,),
        in_specs=in_specs,
        out_specs=out_specs,
        scratch_shapes=scratch,
    )
    outs = pl.pallas_call(
        functools.partial(_moba_sample_body, n_pages=n_pages, page_size=page_size, n_new=n_new,
                          route=route is not None),
        grid_spec=grid_spec,
        out_shape=out_shape,
        compiler_params=_params(1),
        name="moba_sample_route" if route is not None else "moba_sample",
    )(page_table, *args)
    return outs[0], (tuple(outs[1:]) if route is not None else None)


def _memkv_body(mem_ref, mn_ref, w_ref, mkn_ref, k_ref, v_ref):
    xn = _rms(mem_ref[...], mn_ref[...]).astype(BF16)
    kv = jnp.dot(xn, w_ref[...], preferred_element_type=F32)
    mkn = mkn_ref[...]
    for h in range(MEM_HEADS):
        sl = slice(h * MEM_HEAD_DIM, (h + 1) * MEM_HEAD_DIM)
        k_ref[:, sl] = _rms(kv[:, sl], mkn)
    v_ref[...] = kv[:, D_MEM:]


def _memkv(mem2, mem_norm, w_mem_kv, mk_norm):
    n_rows = mem2.shape[0]
    tm = N_MEM
    spec = pl.BlockSpec((tm, D_MEM), lambda i: (i, 0))
    return pl.pallas_call(
        _memkv_body,
        grid=(n_rows // tm,),
        in_specs=[pl.BlockSpec((tm, D_MODEL), lambda i: (i, 0)), _resident((1, D_MODEL)),
                  _resident((D_MODEL, 2 * D_MEM)), _resident((1, MEM_HEAD_DIM))],
        out_specs=(spec, spec),
        out_shape=(jax.ShapeDtypeStruct((n_rows, D_MEM), F32),) * 2,
        compiler_params=_params(1),
        name="memkv",
    )(mem2, mem_norm, w_mem_kv, mk_norm)


def _memattn_body(mq_ref, k_ref, v_ref, o_ref, *, groups, rows, heads_on_lanes):
    scale = MEM_HEAD_DIM ** -0.5
    for g in range(groups):
        rs = slice(g * rows, (g + 1) * rows)
        for h in range(MEM_HEADS):
            sl = slice(h * MEM_HEAD_DIM, (h + 1) * MEM_HEAD_DIM)
            if heads_on_lanes:
                kh, vh = k_ref[g, :, sl], v_ref[g, :, sl]
            else:
                kh, vh = k_ref[g, :, h, :], v_ref[g, :, h, :]
            qh = mq_ref[rs, sl].astype(BF16)
            s = _nt_dot(qh, kh.astype(BF16)) * scale
            p = jnp.exp(s - jnp.max(s, axis=1, keepdims=True))
            l = jnp.sum(p, axis=1, keepdims=True)
            o = jnp.dot(p.astype(BF16), vh.astype(BF16), preferred_element_type=F32)
            o_ref[rs, sl] = (o / l).astype(o_ref.dtype)


def _memattn(mq2, mem_k, mem_v, mem_spec, *, groups, rows):
    n_tok = mq2.shape[0]
    tm = groups * rows
    row_spec = pl.BlockSpec((tm, D_MEM), lambda i: (i, 0))
    heads_on_lanes = sum(d is not None for d in mem_spec.block_shape) == 3
    return pl.pallas_call(
        functools.partial(_memattn_body, groups=groups, rows=rows, heads_on_lanes=heads_on_lanes),
        grid=(n_tok // tm,),
        in_specs=[row_spec, mem_spec, mem_spec],
        out_specs=row_spec,
        out_shape=jax.ShapeDtypeStruct((n_tok, D_MEM), BF16),
        compiler_params=_params(1),
        name="memattn",
    )(mq2, mem_k, mem_v)


def _post_body(x_ref, cy_ref, o_ref, om_ref, gate_ref, wco_ref, wao_ref, wmo_ref, wo_ref,
               fn_ref, wq_ref, h_ref, xn_ref, qh_ref):
    conv_out = jnp.dot(cy_ref[...], wco_ref[...], preferred_element_type=F32)
    attn_out = jnp.dot(o_ref[...], wao_ref[...], preferred_element_type=F32)
    mem_out = jnp.dot(om_ref[...], wmo_ref[...], preferred_element_type=F32)
    merged = (jax.nn.sigmoid(gate_ref[:, 0:D_MODEL]) * conv_out
              + jax.nn.sigmoid(gate_ref[:, D_MODEL:2 * D_MODEL]) * attn_out
              + jax.nn.sigmoid(gate_ref[:, 2 * D_MODEL:3 * D_MODEL]) * mem_out)
    h = x_ref[...] + jnp.dot(merged.astype(BF16), wo_ref[...], preferred_element_type=F32)
    h_ref[...] = h
    xn = _rms(h, fn_ref[...]).astype(BF16)
    xn_ref[...] = xn
    qh = jnp.dot(xn, wq_ref[...], preferred_element_type=F32)
    for hc in range(2 * PEER_HEADS):
        qh_ref[hc] = qh[:, hc * PEER_DKEY:(hc + 1) * PEER_DKEY].astype(BF16)


def _post(x2, cy, o2, om, gates, wco, wao, wmo, wo, ffn_norm, wq):
    n_tok = x2.shape[0]
    tm = min(TOKEN_TILE, n_tok)

    def rows(width):
        return pl.BlockSpec((tm, width), lambda i: (i, 0))

    return pl.pallas_call(
        _post_body,
        grid=(n_tok // tm,),
        in_specs=[rows(D_MODEL), rows(D_CONV), rows(D_ATTN), rows(D_MEM), rows(3 * D_MODEL),
                  _resident(wco.shape), _resident(wao.shape), _resident(wmo.shape),
                  _resident(wo.shape), _resident((1, D_MODEL)), _resident(wq.shape)],
        out_specs=(rows(D_MODEL), rows(D_MODEL),
                   pl.BlockSpec((2 * PEER_HEADS, tm, PEER_DKEY), lambda i: (0, i, 0))),
        out_shape=(jax.ShapeDtypeStruct((n_tok, D_MODEL), F32),
                   jax.ShapeDtypeStruct((n_tok, D_MODEL), BF16),
                   jax.ShapeDtypeStruct((2 * PEER_HEADS, n_tok, PEER_DKEY), BF16)),
        compiler_params=_params(1),
        name="post",
    )(x2, cy, o2, om, gates, wco, wao, wmo, wo, ffn_norm, wq)


_CAND_PAIRS = tuple((a, b) for a in range(PEER_TOPK) for b in range(PEER_TOPK)
                    if (a + 1) * (b + 1) <= PEER_TOPK)


def _tree_argmax(val, idx):
    chunk = 2 * SUBLANES
    if val.shape[0] > chunk:
        parts = [_tree_argmax(val[c:c + chunk], idx[c:c + chunk])
                 for c in range(0, val.shape[0], chunk)]
        val = jnp.stack([p[0] for p in parts])
        idx = jnp.stack([p[1] for p in parts])
    while val.shape[0] > 1:
        n = val.shape[0]
        pairs = n // 2
        v2 = val[:2 * pairs].reshape((pairs, 2) + val.shape[1:])
        i2 = idx[:2 * pairs].reshape((pairs, 2) + idx.shape[1:])
        left = v2[:, 0] >= v2[:, 1]
        new_val = jnp.maximum(v2[:, 0], v2[:, 1])
        new_idx = jnp.where(left, i2[:, 0], i2[:, 1])
        if n % 2:
            new_val = jnp.concatenate([new_val, val[2 * pairs:]], axis=0)
            new_idx = jnp.concatenate([new_idx, idx[2 * pairs:]], axis=0)
        val, idx = new_val, new_idx
    return val[0], idx[0]


def _oddeven_merge(lo, hi, r):
    step = r * 2
    if step < hi - lo:
        yield from _oddeven_merge(lo, hi, step)
        yield from _oddeven_merge(lo + r, hi, step)
        yield from [(i, i + r) for i in range(lo + r, hi - r, step)]
    else:
        yield (lo, lo + r)


def _oddeven_merge_sort(lo, hi):
    if hi - lo >= 1:
        mid = lo + (hi - lo) // 2
        yield from _oddeven_merge_sort(lo, mid)
        yield from _oddeven_merge_sort(mid + 1, hi)
        yield from _oddeven_merge(lo, hi, 1)


_SORT_K = tuple(_oddeven_merge_sort(0, PEER_TOPK - 1))


def _top_k_sorted(vals, ids):
    k = PEER_TOPK
    n = vals.shape[0]
    assert n % k == 0 and (n // k) & (n // k - 1) == 0

    def exchange(v, i, a, b, keep_low=True):
        first = (v[a] > v[b]) | ((v[a] == v[b]) & (i[a] < i[b]))
        hi_i = jnp.where(first, i[a], i[b])
        if keep_low:
            lo_i = jnp.where(first, i[b], i[a])
            v[a], v[b] = jnp.maximum(v[a], v[b]), jnp.minimum(v[a], v[b])
            i[a], i[b] = hi_i, lo_i
        else:
            v[a], i[a] = jnp.maximum(v[a], v[b]), hi_i

    groups = []
    for g in range(n // k):
        v = [vals[g * k + r] for r in range(k)]
        i = [ids[g * k + r] for r in range(k)]
        for a, b in _SORT_K:
            exchange(v, i, a, b)
        groups.append((v, i))
    while len(groups) > 1:
        merged = []
        for (va, ia), (vb, ib) in zip(groups[0::2], groups[1::2]):
            v = va + vb
            i = ia + ib
            for r in range(k):
                exchange(v, i, r, 2 * k - 1 - r, keep_low=False)
            v, i = v[:k], i[:k]
            d = k // 2
            while d >= 1:
                for r in range(k):
                    if r & d == 0:
                        exchange(v, i, r, r + d)
                d //= 2
            merged.append((v, i))
        groups = merged
    return groups[0]


def _route_scratch(tm):
    tile = (tm // LANES, LANES)
    keys = pltpu.VMEM((PEER_NKEYS,) + tile, F32)
    small = pltpu.VMEM((PEER_TOPK,) + tile, F32)
    slots = pltpu.VMEM((PEER_HEADS * PEER_TOPK,) + tile, F32)
    return [keys, keys, small, small, small, small, small, small, slots, slots, slots]


def _route_head(h, qh1_ref, qh2_ref, sub1_ref, sub2_ref, scratch):
    (x1_ref, x2_ref, st1_ref, ix1_ref, st2_ref, ix2_ref, best_ref, flat_ref,
     si_ref, sj_ref, sg_ref) = scratch
    n_sub = x1_ref.shape[1]
    k = PEER_TOPK
    tile = (n_sub, LANES)
    key_id = lax.broadcasted_iota(jnp.int32, (PEER_NKEYS,) + tile, 0).astype(F32)
    flat_id = jnp.stack([jnp.full(tile, float(a * k + b), F32) for a, b in _CAND_PAIRS])

    def scores_into(sub_ref, qh_ref, x_ref):
        scores = _nt_dot(sub_ref[...], qh_ref[...])
        for s in range(n_sub):
            x_ref[:, s, :] = scores[:, s * LANES:(s + 1) * LANES]

    def top_keys(x_ref, st_ref, ix_ref):
        v, i = _top_k_sorted(x_ref[...], key_id)
        st_ref[...] = jnp.stack(v)
        ix_ref[...] = jnp.stack(i)

    def lookup(rank, table):
        out = jnp.zeros(rank.shape, F32)
        for a in range(k):
            out = jnp.where(rank == float(a), table[a][None], out)
        return out

    scores_into(sub1_ref, qh1_ref, x1_ref)
    scores_into(sub2_ref, qh2_ref, x2_ref)
    top_keys(x1_ref, st1_ref, ix1_ref)
    top_keys(x2_ref, st2_ref, ix2_ref)
    st1 = st1_ref[...]
    st2 = st2_ref[...]
    cand = jnp.stack([st1[a] + st2[b] for a, b in _CAND_PAIRS])

    def rnd2(r, x):
        m, f = _tree_argmax(x, flat_id)
        best_ref[r] = m
        flat_ref[r] = f
        return jnp.where(flat_id == f[None], NEG_INF, x)

    lax.fori_loop(0, k, rnd2, cand)
    best = best_ref[...]
    flat = flat_ref[...]
    rank_a = jnp.floor(flat * (1.0 / k))
    rank_b = flat - rank_a * k
    e = jnp.exp(best - best[0:1])
    rows = pl.ds(pl.multiple_of(h * k, k), k)
    si_ref[rows] = lookup(rank_a, ix1_ref[...])
    sj_ref[rows] = lookup(rank_b, ix2_ref[...])
    sg_ref[rows] = e / jnp.sum(e, axis=0, keepdims=True)


def _route_emit(scratch, i_ref, j_ref, g_ref):
    si_ref, sj_ref, sg_ref = scratch[-3:]
    for s in range(si_ref.shape[1]):
        rows = slice(s * LANES, (s + 1) * LANES)
        i_ref[rows, :] = si_ref[:, s, :].T
        j_ref[rows, :] = sj_ref[:, s, :].T
        g_ref[rows, :] = sg_ref[:, s, :].T


def _route_body(qh_ref, sub_ref, i_ref, j_ref, g_ref, *scratch):
    def head(h, carry):
        _route_head(h, qh_ref.at[2 * h], qh_ref.at[2 * h + 1], sub_ref.at[2 * h],
                    sub_ref.at[2 * h + 1], scratch)
        return carry

    lax.fori_loop(0, PEER_HEADS, head, 0)
    _route_emit(scratch, i_ref, j_ref, g_ref)


def _route(qh3, subkeys):
    n_tok = qh3.shape[1]
    tm = min(ROUTE_TILE, n_tok)
    n_slots = PEER_HEADS * PEER_TOPK
    assert n_slots == LANES and tm % LANES == 0 and n_tok % tm == 0
    sub3 = subkeys.reshape(2 * PEER_HEADS, PEER_NKEYS, PEER_DKEY).astype(BF16)
    slot_spec = pl.BlockSpec((tm, n_slots), lambda i: (i, 0))
    return pl.pallas_call(
        _route_body,
        grid=(n_tok // tm,),
        in_specs=[pl.BlockSpec((2 * PEER_HEADS, tm, PEER_DKEY), lambda i: (0, i, 0)),
                  _resident(sub3.shape)],
        out_specs=(slot_spec,) * 3,
        out_shape=(jax.ShapeDtypeStruct((n_tok, n_slots), F32),) * 3,
        scratch_shapes=_route_scratch(tm),
        compiler_params=_params(1),
        name="peer_route",
    )(qh3, sub3)


WBUILD_GROUP = 16
WBUILD_PITCH = PEER_NKEYS + SUBLANES


def _wbuild_body(i_ref, j_ref, g_ref, w_ref, stage_a_ref, stage_b_ref):
    tm = i_ref.shape[0]
    key = lax.broadcasted_iota(jnp.int32, (PEER_NKEYS, LANES), 0).astype(F32)

    def build(gi, stage_ref):
        base = pl.multiple_of(gi * WBUILD_GROUP, WBUILD_GROUP)
        for tt in range(WBUILD_GROUP):
            irow = i_ref[pl.ds(base + tt, 1), :]
            jrow = j_ref[pl.ds(base + tt, 1), :]
            grow = g_ref[pl.ds(base + tt, 1), :]
            a = jnp.where(key == irow, grow, 0.0).astype(BF16)
            b = jnp.where(key == jrow, 1.0, 0.0).astype(BF16)
            stage_ref[tt * WBUILD_PITCH:tt * WBUILD_PITCH + PEER_NKEYS, :] = _nt_dot(a, b)

    def drain(gi, stage_ref):
        base = pl.multiple_of(gi * WBUILD_GROUP, WBUILD_GROUP)
        for i in range(PEER_NKEYS):
            tile = stage_ref[pl.ds(i, WBUILD_GROUP, stride=WBUILD_PITCH), :]
            w_ref[pl.ds(base, WBUILD_GROUP), i * PEER_NKEYS:(i + 1) * PEER_NKEYS] = tile.astype(BF16)

    n_groups = tm // WBUILD_GROUP
    assert n_groups % 2 == 0
    build(0, stage_a_ref)

    def step(p, carry):
        build(2 * p + 1, stage_b_ref)
        drain(2 * p, stage_a_ref)
        build(2 * p + 2, stage_a_ref)
        drain(2 * p + 1, stage_b_ref)
        return carry

    lax.fori_loop(0, n_groups // 2 - 1, step, 0)
    build(n_groups - 1, stage_b_ref)
    drain(n_groups - 2, stage_a_ref)
    drain(n_groups - 1, stage_b_ref)


def _wbuild(si, sj, sg):
    n_tok = si.shape[0]
    tm = min(WBUILD_TILE, n_tok)
    assert tm % WBUILD_GROUP == 0
    slot_spec = pl.BlockSpec((tm, LANES), lambda i: (i, 0))
    return pl.pallas_call(
        _wbuild_body,
        grid=(n_tok // tm,),
        in_specs=[slot_spec] * 3,
        out_specs=pl.BlockSpec((tm, PEER_N_EXPERTS), lambda i: (i, 0)),
        out_shape=jax.ShapeDtypeStruct((n_tok, PEER_N_EXPERTS), BF16),
        scratch_shapes=[pltpu.VMEM((WBUILD_GROUP * WBUILD_PITCH, PEER_NKEYS), F32)] * 2,
        compiler_params=_params(1),
        name="peer_wbuild",
    )(si, sj, sg)


def _peer_body(xn_ref, u_ref, v_ref, w_ref, h_ref, y_ref):
    @pl.when(pl.program_id(1) == 0)
    def _():
        y_ref[...] = h_ref[...]

    xn = xn_ref[...]
    sub = PEER_EXPERT_SUB
    for s in range(u_ref.shape[0] // sub):
        sl = slice(s * sub, (s + 1) * sub)
        z = _nt_dot(xn, u_ref[sl, :])
        a = 0.5 * z * (1.0 + lax.erf(z * (2.0 ** -0.5)))
        a = a * w_ref[:, sl].astype(F32)
        y_ref[...] += jnp.dot(a.astype(BF16), v_ref[sl, :], preferred_element_type=F32)


def _peer_dense(xn, w2, h, peer_u, peer_v):
    n_tok = xn.shape[0]
    tm = min(PEER_TOKEN_TILE, n_tok)
    ec = PEER_EXPERT_CHUNK
    assert n_tok % tm == 0 and PEER_N_EXPERTS % ec == 0 and ec % PEER_EXPERT_SUB == 0
    tok_spec = pl.BlockSpec((tm, D_MODEL), lambda i, c: (i, 0))
    exp_spec = pl.BlockSpec((ec, D_MODEL), lambda i, c: (c, 0))
    return pl.pallas_call(
        _peer_body,
        grid=(n_tok // tm, PEER_N_EXPERTS // ec),
        in_specs=[tok_spec, exp_spec, exp_spec, pl.BlockSpec((tm, ec), lambda i, c: (i, c)),
                  pl.BlockSpec((tm, D_MODEL), lambda i, c: (i, 0), pipeline_mode=pl.Buffered(1))],
        out_specs=tok_spec,
        out_shape=jax.ShapeDtypeStruct((n_tok, D_MODEL), F32),
        compiler_params=_params(2),
        name="peer_dense",
    )(xn, peer_u, peer_v, w2, h)


def _row(v):
    return v.reshape(1, -1)


def _peer(h, xn, qh3, lw, slots=None):
    if slots is None:
        slots = _route(qh3, lw["subkeys"])
    w2 = _wbuild(*slots)
    return _peer_dense(xn, w2, h, lw["peer_u"], lw["peer_v"])


def kernel(x_prompt, x_sample, mem_prompt, cache_k, cache_v, cache_conv, cache_mem_k, cache_mem_v,
           page_table, attn_norm, w_in, conv_w, w_conv_out, q_norm, k_norm, w_attn_out,
           mem_norm, w_mem_kv, mk_norm, mq_norm, w_mem_out, w_o,
           ffn_norm, peer_wq, peer_subkeys, peer_u, peer_v):
    n_p, t_p, _ = x_prompt.shape
    n_s, t_s, _ = x_sample.shape
    depth = attn_norm.shape[0]
    past_len = page_table.shape[1] * cache_k.shape[2]
    h_p = x_prompt.reshape(n_p * t_p, D_MODEL)
    h_s = x_sample.reshape(n_s * t_s, D_MODEL)
    outs = [[] for _ in range(8)]
    for l in range(depth):
        lw = dict(
            wco=w_conv_out[l].astype(BF16), wao=w_attn_out[l].astype(BF16),
            wmo=w_mem_out[l].astype(BF16), wo=w_o[l].astype(BF16), wq=peer_wq[l].astype(BF16),
            ffn_norm=_row(ffn_norm[l]), subkeys=peer_subkeys[l],
            peer_u=peer_u[l].astype(BF16), peer_v=peer_v[l].astype(BF16))
        in_w = (_row(attn_norm[l]), w_in[l].astype(BF16), _row(q_norm[l]), _row(k_norm[l]),
                _row(mq_norm[l]), conv_w[l])

        mem_k, mem_v = _memkv(mem_prompt.reshape(n_p * N_MEM, D_MODEL), _row(mem_norm[l]),
                              w_mem_kv[l].astype(BF16), _row(mk_norm[l]))
        zero_state = jnp.zeros((SUBLANES, D_CONV), F32)
        cy, u, q, k, v, mq, gates, kmean = _inproj(h_p, *in_w, zero_state, zero_state, seq_len=t_p)
        o = _moba_prompt(q, k, v, kmean, n_seq=n_p, seq_len=t_p)
        steps_per_mem = t_p // TOKEN_TILE

        prompt_mem = pl.BlockSpec((1, N_MEM, D_MEM), lambda i: (i // steps_per_mem, 0, 0))
        om = _memattn(mq, mem_k.reshape(n_p, N_MEM, D_MEM), mem_v.reshape(n_p, N_MEM, D_MEM),
                      prompt_mem, groups=1, rows=TOKEN_TILE)
        h_mid_p, xn_p, qh3_p = _post(h_p, cy, o, om, gates, lw["wco"], lw["wao"], lw["wmo"], lw["wo"],
                                     lw["ffn_norm"], lw["wq"])
        outs[0].append(k.reshape(n_p, t_p, N_HEADS, HEAD_DIM))
        outs[1].append(v.reshape(n_p, t_p, N_HEADS, HEAD_DIM))
        outs[2].append(u.reshape(n_p, t_p, D_CONV)[:, t_p - (CONV_W - 1):])
        outs[3].append(mem_k.reshape(n_p, N_MEM, MEM_HEADS, MEM_HEAD_DIM))
        outs[4].append(mem_v.reshape(n_p, N_MEM, MEM_HEADS, MEM_HEAD_DIM))

        state = cache_conv[l]
        s0x = jnp.repeat(state[:, 0], t_s, axis=0)
        s1x = jnp.repeat(state[:, 1], t_s, axis=0)
        cy, u, q, k, v, mq, gates, _ = _inproj(h_s, *in_w, s0x, s1x, seq_len=t_s)
        k4 = k.reshape(n_s, t_s, N_HEADS, HEAD_DIM)
        v4 = v.reshape(n_s, t_s, N_HEADS, HEAD_DIM)
        rows = n_s * t_s * N_HEADS
        ride = (n_p * t_p) % ROUTE_TILE == 0 and (n_p * t_p // ROUTE_TILE) * PEER_HEADS == n_s
        o, slots_p = _moba_sample(q.reshape(rows, HEAD_DIM), k4.reshape(rows, HEAD_DIM),
                                  v4.reshape(rows, HEAD_DIM), cache_k, cache_v, page_table, l,
                                  n_seq=n_s, n_new=t_s, past_len=past_len,
                                  route=(qh3_p, lw["subkeys"]) if ride else None)
        h_p = _peer(h_mid_p, xn_p, qh3_p, lw, slots=slots_p)
        o = o.reshape(n_s * t_s, D_ATTN)
        groups = SUBLANES

        sample_mem = pl.BlockSpec((None, groups, N_MEM, MEM_HEADS, MEM_HEAD_DIM),
                                  lambda i: (l, i, 0, 0, 0))
        om = _memattn(mq, cache_mem_k, cache_mem_v, sample_mem, groups=groups, rows=t_s)
        h_mid_s, xn_s, qh3_s = _post(h_s, cy, o, om, gates, lw["wco"], lw["wao"], lw["wmo"], lw["wo"],
                                     lw["ffn_norm"], lw["wq"])
        h_s = _peer(h_mid_s, xn_s, qh3_s, lw)
        outs[5].append(k4)
        outs[6].append(v4)
        ext = jnp.concatenate([state, u.reshape(n_s, t_s, D_CONV)], axis=1)
        outs[7].append(ext[:, t_s:])
    stacked = [jnp.stack(o) for o in outs]
    return (h_p.reshape(n_p, t_p, D_MODEL), h_s.reshape(n_s, t_s, D_MODEL), *stacked)
```
